```python
import math, functools
import jax, jax.numpy as jnp
from jax import lax
import numpy as np

D_MODEL = 1024
BATCH = 2
SEQ = 8192
DEPTH = 2
DEC_BATCH = 32
DEC_SEQ = 4
PAST_LEN = 16384
PAGE_SIZE = 128

GROUP_W = D_MODEL // 4
LRU_HEADS = 4
LRU_HD = GROUP_W // LRU_HEADS
CONV_W = 4
RG_C = 8.0
CHUNK = 128
G_HEADS = 4
G_HD = GROUP_W // G_HEADS
POOL_WINDOWS = (2, 4, 8, 16)
POOL_GROUPS = 4
POOL_GD = GROUP_W // POOL_GROUPS
POOL_MAX = 16
ATT_HEADS = 4
HEAD_DIM = GROUP_W // ATT_HEADS
IDX_HEADS = 8
IDX_DIM = 64
TOPK_MAX = 256
QBLK = 128
NUM_BUCKETS = 32
MAX_DISTANCE = 128
D_FF = 4 * D_MODEL
ALPHA = (2 * DEPTH) ** 0.25
BETA = (8 * DEPTH) ** -0.25
LN_EPS = 1e-5
SPLITS = (GROUP_W, GROUP_W, GROUP_W, GROUP_W, GROUP_W, GROUP_W, GROUP_W, GROUP_W,
          IDX_HEADS * IDX_DIM, IDX_DIM, IDX_HEADS)
D_IN = sum(SPLITS)

kernel_name = "hybrid_lru_gmlp_pool_dsa_decode_step"


def layer_norm(x, g, b):
    xf = x.astype(jnp.float32)
    mu = jnp.mean(xf, axis=-1, keepdims=True)
    var = jnp.mean(jnp.square(xf - mu), axis=-1, keepdims=True)
    return ((xf - mu) * lax.rsqrt(var + LN_EPS) * g + b).astype(x.dtype)


def t5_bucket(dist):
    max_exact = NUM_BUCKETS // 2
    n = jnp.maximum(dist, 0)
    nf = jnp.maximum(n, 1).astype(jnp.float32)
    large = max_exact + (jnp.log(nf / max_exact) / math.log(MAX_DISTANCE / max_exact)
                         * (NUM_BUCKETS - max_exact)).astype(jnp.int32)
    large = jnp.minimum(large, NUM_BUCKETS - 1)
    return jnp.where(n < max_exact, n, large)


def rglru_mixer(xa, ga, conv_buf, h0, conv_w, conv_b, wa, ba, wx, bx, lam):
    B, T, C = xa.shape
    x_ext = jnp.concatenate([conv_buf.astype(xa.dtype), xa], axis=1)
    xc = sum(x_ext[:, j:j + T] * conv_w[j] for j in range(CONV_W)) + conv_b
    new_buf = x_ext[:, -(CONV_W - 1):]
    xh = xc.reshape(B, T, LRU_HEADS, LRU_HD)
    r = jax.nn.sigmoid((jnp.einsum('bthi,hij->bthj', xh, wa).reshape(B, T, C) + ba).astype(jnp.float32))
    i = jax.nn.sigmoid((jnp.einsum('bthi,hij->bthj', xh, wx).reshape(B, T, C) + bx).astype(jnp.float32))
    log_a = -RG_C * r * jax.nn.softplus(-lam.astype(jnp.float32))
    a = jnp.exp(log_a)
    mult = jnp.sqrt(-jnp.expm1(2.0 * log_a))
    b = mult * i * xc.astype(jnp.float32)
    b = b.at[:, 0].add(a[:, 0] * h0.astype(jnp.float32))

    def combine(c1, c2):
        a1, b1 = c1
        a2, b2 = c2
        return a1 * a2, a2 * b1 + b2

    _, h = lax.associative_scan(combine, (a, b), axis=1)
    y = h.astype(xa.dtype) * jax.nn.gelu(ga)
    return y, new_buf, h[:, -1].astype(xa.dtype)


def gmlp_mixer(ub, vb, ln_g, ln_b, ws, bs):
    B, T, C = ub.shape
    u = jax.nn.gelu(ub)
    vn = layer_norm(jax.nn.gelu(vb), ln_g, ln_b)
    pad = (-T) % CHUNK
    vp = jnp.pad(vn, ((0, 0), (0, pad), (0, 0)))
    nc = (T + pad) // CHUNK
    vc = vp.reshape(B, nc, CHUNK, G_HEADS, G_HD)
    w = ws * jnp.tril(jnp.ones((CHUNK, CHUNK), ws.dtype))
    mixed = jnp.einsum('hts,bcshd->bcthd', w, vc) + bs.T[None, None, :, :, None]
    mixed = mixed.reshape(B, T + pad, C)[:, :T]
    return u * mixed, vn


def pool_mixer(xc, pool_buf, start_pos, pool_w, pool_scale):
    B, T, C = xc.shape
    P = POOL_MAX - 1
    x_ext = jnp.concatenate([pool_buf.astype(xc.dtype), xc], axis=1)
    xf = x_ext.astype(jnp.float32)
    cs0 = jnp.concatenate([jnp.zeros((B, 1, C), jnp.float32), jnp.cumsum(xf, axis=1)], axis=1)
    pos = start_pos + jnp.arange(T)
    means = []
    for g, w in enumerate(POOL_WINDOWS):
        sl = slice(g * POOL_GD, (g + 1) * POOL_GD)
        s = cs0[:, P + 1:P + 1 + T, sl] - cs0[:, P + 1 - w:P + 1 - w + T, sl]
        cnt = jnp.minimum(pos + 1, w).astype(jnp.float32)
        means.append(s / cnt[None, :, None])
    pooled = (jnp.concatenate(means, axis=-1) - xf[:, P:]).astype(xc.dtype)
    out = jnp.einsum('bthi,hij->bthj', pooled.reshape(B, T, POOL_GROUPS, POOL_GD), pool_w)
    out = out.reshape(B, T, C) * pool_scale
    return out, x_ext[:, -P:]


def indexer_scores(qi, wi, ki):
    dots = jax.nn.relu(jnp.einsum('bqhd,bsd->bqhs', qi, ki).astype(jnp.float32)) * (IDX_DIM ** -0.5)
    return jnp.einsum('bqhs,bqh->bqs', dots, wi.astype(jnp.float32)) * (IDX_HEADS ** -0.5)


def sparse_attend(q, k_sel, v_sel, dist, valid, rel_bias):
    logits = jnp.einsum('bqhd,bqkhd->bqhk', q, k_sel).astype(jnp.float32) * (HEAD_DIM ** -0.5)
    bias = rel_bias[t5_bucket(dist)].astype(jnp.float32)
    logits = logits + jnp.transpose(bias, (0, 1, 3, 2))
    logits = jnp.where(valid[:, :, None, :], logits, -jnp.inf)
    p = jax.nn.softmax(logits, axis=-1).astype(v_sel.dtype)
    return jnp.einsum('bqhk,bqkhd->bqhd', p, v_sel)


def dsa_prompt(q, k, v, qi, ki, wi, rel_bias, topk):
    B, T = q.shape[:2]
    nb = T // QBLK
    bidx = jnp.arange(B)[:, None, None]
    kpos = jnp.arange(T)

    def blk(args):
        q_b, qi_b, wi_b, b = args
        qpos = b * QBLK + jnp.arange(QBLK)
        s = indexer_scores(qi_b, wi_b, ki)
        s = jnp.where((kpos[None, :] <= qpos[:, None])[None], s, -jnp.inf)
        _, sel = lax.top_k(s, topk)
        dist = qpos[None, :, None] - sel
        return sparse_attend(q_b, k[bidx, sel], v[bidx, sel], dist, dist >= 0, rel_bias)

    def blocks(a):
        return jnp.swapaxes(a.reshape((B, nb, QBLK) + a.shape[2:]), 0, 1)

    out = lax.map(blk, (blocks(q), blocks(qi), blocks(wi), jnp.arange(nb)))
    return jnp.swapaxes(out, 0, 1).reshape(B, T, ATT_HEADS * HEAD_DIM)


def dsa_sample(q, k, v, qi, ki, wi, cache_k_l, cache_v_l, cache_ik_l, page_table, rel_bias, topk):
    DB, T = q.shape[:2]
    P = page_table.shape[1] * PAGE_SIZE
    past_ki = cache_ik_l[page_table].reshape(DB, P, IDX_DIM)
    keys_i = jnp.concatenate([past_ki, ki.astype(past_ki.dtype)], axis=1)
    s = indexer_scores(qi, wi, keys_i)
    qpos = P + jnp.arange(T)
    kpos = jnp.arange(P + T)
    s = jnp.where((kpos[None, :] <= qpos[:, None])[None], s, -jnp.inf)
    _, sel = lax.top_k(s, topk)
    in_past = sel < P
    ps = jnp.minimum(sel, P - 1)
    bidx = jnp.arange(DB)[:, None, None]
    phys = page_table[bidx, ps // PAGE_SIZE]
    off = ps % PAGE_SIZE
    ns = jnp.clip(sel - P, 0, T - 1)
    k_sel = jnp.where(in_past[..., None, None], cache_k_l[phys, off].astype(k.dtype), k[bidx, ns])
    v_sel = jnp.where(in_past[..., None, None], cache_v_l[phys, off].astype(v.dtype), v[bidx, ns])
    dist = qpos[None, :, None] - sel
    out = sparse_attend(q, k_sel, v_sel, dist, dist >= 0, rel_bias)
    return out.reshape(DB, T, ATT_HEADS * HEAD_DIM)


def hybrid_layer(x, params, conv_buf, h0, pool_buf, start_pos, attn):
    (w_in, conv_w, conv_b, lru_wa, lru_ba, lru_wx, lru_bx, lru_lambda, gmlp_ln_g, gmlp_ln_b,
     gmlp_ws, gmlp_bs, pool_w, pool_scale, w_out, ln1_g, ln1_b, ln2_g, ln2_b, ffn_w1, ffn_w2) = params
    B, T, _ = x.shape
    z = x @ w_in
    split_at = np.cumsum(SPLITS)[:-1].tolist()
    xa, ga, ub, vb, xc, q, k, v, qi, ki, wi = jnp.split(z, split_at, axis=-1)
    ya, conv_new, h_new = rglru_mixer(xa, ga, conv_buf, h0, conv_w, conv_b,
                                      lru_wa, lru_ba, lru_wx, lru_bx, lru_lambda)
    yb, vn = gmlp_mixer(ub, vb, gmlp_ln_g, gmlp_ln_b, gmlp_ws, gmlp_bs)
    yc, pool_new = pool_mixer(xc, pool_buf, start_pos, pool_w, pool_scale)
    q = q.reshape(B, T, ATT_HEADS, HEAD_DIM)
    k = k.reshape(B, T, ATT_HEADS, HEAD_DIM)
    v = v.reshape(B, T, ATT_HEADS, HEAD_DIM)
    qi = qi.reshape(B, T, IDX_HEADS, IDX_DIM)
    yd = attn(q, k, v, qi, ki, wi)
    mix = jnp.concatenate([ya, yb, yc, yd], axis=-1) @ w_out
    x = layer_norm(ALPHA * x + mix, ln1_g, ln1_b)
    hdn = jnp.square(jax.nn.relu(x @ ffn_w1))
    x = layer_norm(ALPHA * x + hdn @ ffn_w2, ln2_g, ln2_b)
    return x, (k, v, ki, h_new, conv_new, pool_new, vn)


def setup_inputs(seed: int = 0) -> dict:
    key = jax.random.key(seed)
    ks = jax.random.split(key, 40)
    nrm = lambda i, shape: jax.random.normal(ks[i], shape, jnp.float32)
    n_pages = PAST_LEN // PAGE_SIZE
    n_used = DEC_BATCH * n_pages
    n_pool = n_used + max(1, n_used // 4)
    page_table = jax.random.permutation(ks[9], n_pool)[:n_used].reshape(DEC_BATCH, n_pages).astype(jnp.int32)
    a0 = jax.random.uniform(ks[17], (DEPTH, GROUP_W), jnp.float32, minval=0.9, maxval=0.999)
    s = a0 ** (1.0 / RG_C)
    lru_lambda = jnp.log(s) - jnp.log1p(-s)
    return {
        "x_prompt": nrm(0, (BATCH, SEQ, D_MODEL)),
        "x_sample": nrm(1, (DEC_BATCH, DEC_SEQ, D_MODEL)),
        "cache_k": nrm(2, (DEPTH, n_pool, PAGE_SIZE, ATT_HEADS, HEAD_DIM)),
        "cache_v": nrm(3, (DEPTH, n_pool, PAGE_SIZE, ATT_HEADS, HEAD_DIM)),
        "cache_idx_k": nrm(4, (DEPTH, n_pool, PAGE_SIZE, IDX_DIM)),
        "state_lru_h": 0.5 * nrm(5, (DEPTH, DEC_BATCH, GROUP_W)),
        "state_conv": nrm(6, (DEPTH, DEC_BATCH, CONV_W - 1, GROUP_W)),
        "state_pool": nrm(7, (DEPTH, DEC_BATCH, POOL_MAX - 1, GROUP_W)),
        "page_table": page_table,
        "w_in": nrm(10, (DEPTH, D_MODEL, D_IN)) * D_MODEL ** -0.5,
        "conv_w": nrm(11, (DEPTH, CONV_W, GROUP_W)) * 0.5,
        "conv_b": 0.01 * nrm(12, (DEPTH, GROUP_W)),
        "lru_wa": nrm(13, (DEPTH, LRU_HEADS, LRU_HD, LRU_HD)) * LRU_HD ** -0.5,
        "lru_ba": 0.01 * nrm(14, (DEPTH, GROUP_W)),
        "lru_wx": nrm(15, (DEPTH, LRU_HEADS, LRU_HD, LRU_HD)) * LRU_HD ** -0.5,
        "lru_bx": 0.01 * nrm(16, (DEPTH, GROUP_W)),
        "lru_lambda": lru_lambda,
        "gmlp_ln_g": 1.0 + 0.01 * nrm(18, (DEPTH, GROUP_W)),
        "gmlp_ln_b": 0.01 * nrm(19, (DEPTH, GROUP_W)),
        "gmlp_ws": nrm(20, (DEPTH, G_HEADS, CHUNK, CHUNK)) * CHUNK ** -0.5,
        "gmlp_bs": 1.0 + 0.1 * nrm(21, (DEPTH, G_HEADS, CHUNK)),
        "pool_w": nrm(22, (DEPTH, POOL_GROUPS, POOL_GD, POOL_GD)) * POOL_GD ** -0.5,
        "pool_scale": 1.0 + 0.1 * nrm(23, (DEPTH, GROUP_W)),
        "rel_bias": 0.5 * nrm(24, (NUM_BUCKETS, ATT_HEADS)),
        "w_out": nrm(25, (DEPTH, D_MODEL, D_MODEL)) * (D_MODEL ** -0.5) * BETA,
        "ln1_g": 1.0 + 0.01 * nrm(26, (DEPTH, D_MODEL)),
        "ln1_b": 0.01 * nrm(27, (DEPTH, D_MODEL)),
        "ln2_g": 1.0 + 0.01 * nrm(28, (DEPTH, D_MODEL)),
        "ln2_b": 0.01 * nrm(29, (DEPTH, D_MODEL)),
        "ffn_w1": nrm(30, (DEPTH, D_MODEL, D_FF)) * (D_MODEL ** -0.5) * BETA,
        "ffn_w2": nrm(31, (DEPTH, D_FF, D_MODEL)) * (D_FF ** -0.5) * BETA,
    }


def stack_layers(states, i):
    return jnp.stack([s[i] for s in states], axis=0)


def reference(x_prompt, x_sample, cache_k, cache_v, cache_idx_k, state_lru_h, state_conv, state_pool,
              page_table, w_in, conv_w, conv_b, lru_wa, lru_ba, lru_wx, lru_bx, lru_lambda,
              gmlp_ln_g, gmlp_ln_b, gmlp_ws, gmlp_bs, pool_w, pool_scale, rel_bias, w_out,
              ln1_g, ln1_b, ln2_g, ln2_b, ffn_w1, ffn_w2):
    B, T = x_prompt.shape[:2]
    DB, TS = x_sample.shape[:2]
    past_len = page_table.shape[1] * PAGE_SIZE
    topk_p = min(TOPK_MAX, T // 4)
    topk_s = min(TOPK_MAX, (past_len + TS) // 4)
    conv0 = jnp.zeros((B, CONV_W - 1, GROUP_W), x_prompt.dtype)
    h0 = jnp.zeros((B, GROUP_W), x_prompt.dtype)
    pool0 = jnp.zeros((B, POOL_MAX - 1, GROUP_W), x_prompt.dtype)
    xp, xs = x_prompt, x_sample
    st_p, st_s = [], []
    for l in range(DEPTH):
        p_l = (w_in[l], conv_w[l], conv_b[l], lru_wa[l], lru_ba[l], lru_wx[l], lru_bx[l], lru_lambda[l],
               gmlp_ln_g[l], gmlp_ln_b[l], gmlp_ws[l], gmlp_bs[l], pool_w[l], pool_scale[l], w_out[l],
               ln1_g[l], ln1_b[l], ln2_g[l], ln2_b[l], ffn_w1[l], ffn_w2[l])
        attn_p = functools.partial(dsa_prompt, rel_bias=rel_bias, topk=topk_p)
        attn_s = functools.partial(dsa_sample, cache_k_l=cache_k[l], cache_v_l=cache_v[l],
                                   cache_ik_l=cache_idx_k[l], page_table=page_table,
                                   rel_bias=rel_bias, topk=topk_s)
        xp, sp = hybrid_layer(xp, p_l, conv0, h0, pool0, 0, attn_p)
        xs, ss = hybrid_layer(xs, p_l, state_conv[l], state_lru_h[l], state_pool[l], past_len, attn_s)
        st_p.append(sp)
        st_s.append(ss)
    k_p, v_p, ik_p = stack_layers(st_p, 0), stack_layers(st_p, 1), stack_layers(st_p, 2)
    h_p, conv_p, pool_p = stack_layers(st_p, 3), stack_layers(st_p, 4), stack_layers(st_p, 5)
    k_s, v_s, ik_s = stack_layers(st_s, 0), stack_layers(st_s, 1), stack_layers(st_s, 2)
    h_s, conv_s, pool_s = stack_layers(st_s, 3), stack_layers(st_s, 4), stack_layers(st_s, 5)
    gv_s = stack_layers(st_s, 6)
    return (xp, xs, k_p, v_p, ik_p, h_p, conv_p, pool_p, k_s, v_s, ik_s, h_s, conv_s, pool_s, gv_s)
```

```python
import functools
import math

import jax
import jax.numpy as jnp
from jax import lax
from jax.experimental import pallas as pl
from jax.experimental.pallas import tpu as pltpu

F32 = jnp.float32
BF16 = jnp.bfloat16
I32 = jnp.int32

GROUP_W = 256
LRU_HEADS = 4
CONV_W = 4
RG_C = 8.0
CHUNK = 128
G_HEADS = 4
POOL_WINDOWS = (2, 4, 8, 16)
POOL_MAX = 16
ATT_HEADS = 4
HEAD_DIM = 64
IDX_HEADS = 8
IDX_DIM = 64
TOPK_MAX = 256
NUM_BUCKETS = 32
MAX_DISTANCE = 128
PAGE_SIZE = 128
LN_EPS = 1e-5

LANES = 128
SUBLANES = 8
KEY_CHUNK = 128
VMEM_LIMIT = 56 * 1024 * 1024

INT_MIN = -2 ** 31
NEG_BIG = -1e30
IDX_SCALE = (IDX_DIM ** -0.5) * (IDX_HEADS ** -0.5)
ATT_SCALE = HEAD_DIM ** -0.5


def _round_up(n, m):
    return (n + m - 1) // m * m


def _gelu(x):
    return x * (0.5 * (1.0 + jnp.tanh(math.sqrt(2.0 / math.pi) * (x + 0.044715 * (x * x * x)))))


def _layer_norm(x, g, b):
    mu = jnp.mean(x, axis=-1, keepdims=True)
    var = jnp.mean(jnp.square(x - mu), axis=-1, keepdims=True)
    return (x - mu) * lax.rsqrt(var + LN_EPS) * g + b


def _dot(a, b):
    return jnp.dot(a, b, preferred_element_type=F32)


def _dot_nt(a, b):
    return lax.dot_general(a, b, (((1,), (1,)), ((), ())), preferred_element_type=F32)


def _score_key(s):
    s = jnp.where(s == 0.0, 0.0, s)
    bits = lax.bitcast_convert_type(s, I32)
    return bits ^ (lax.shift_right_arithmetic(bits, 31) & 0x7FFFFFFF)


def _div_pow2(x, d):
    shift = d.bit_length() - 1
    assert 1 << shift == d
    return lax.shift_right_logical(x, shift) if shift else x


def _lane_head(shape, width):
    return _div_pow2(lax.broadcasted_iota(I32, shape, len(shape) - 1), width)


def _bias_kernel(rel_ref, tabp_ref, tabs_ref):
    def bucket(dist):
        max_exact = NUM_BUCKETS // 2
        n = jnp.maximum(dist, 0)
        nf = jnp.maximum(n, 1).astype(F32)
        large = max_exact + (jnp.log(nf / max_exact) / math.log(MAX_DISTANCE / max_exact)
                             * (NUM_BUCKETS - max_exact)).astype(I32)
        large = jnp.minimum(large, NUM_BUCKETS - 1)
        return jnp.where(n < max_exact, n, large)

    def lookup(bk, h):
        acc = jnp.zeros(bk.shape, F32)
        for j in range(NUM_BUCKETS):
            acc = jnp.where(bk == j, rel_ref[j * ATT_HEADS + h], acc)
        return acc

    r = lax.broadcasted_iota(I32, (KEY_CHUNK, KEY_CHUNK), 0)
    c = lax.broadcasted_iota(I32, (KEY_CHUNK, KEY_CHUNK), 1)
    b_prev = bucket(KEY_CHUNK + r - c)
    b_diag = bucket(r - c)
    t = lax.broadcasted_iota(I32, (SUBLANES, KEY_CHUNK), 0)
    cs = lax.broadcasted_iota(I32, (SUBLANES, KEY_CHUNK), 1)
    b_last = bucket(PAGE_SIZE + t - cs)
    b_new = bucket(t - cs)
    for h in range(ATT_HEADS):
        tabp_ref[h, 0] = lookup(b_prev, h)
        tabp_ref[h, 1] = lookup(b_diag, h)
        rows = pl.ds(h * SUBLANES, SUBLANES)
        tabs_ref[0, rows, :] = lookup(b_last, h)
        tabs_ref[1, rows, :] = lookup(b_new, h)
        tabs_ref[2, rows, :] = jnp.full((SUBLANES, KEY_CHUNK), rel_ref[(NUM_BUCKETS - 1) * ATT_HEADS + h], F32)


def _bias_tables(rel_bias):
    return pl.pallas_call(
        _bias_kernel,
        out_shape=(jax.ShapeDtypeStruct((ATT_HEADS, 2, KEY_CHUNK, KEY_CHUNK), F32),
                   jax.ShapeDtypeStruct((3, ATT_HEADS * SUBLANES, KEY_CHUNK), F32)),
        in_specs=[pl.BlockSpec(memory_space=pltpu.SMEM)],
        name="bias_tables",
    )(rel_bias.reshape(-1))


def _mixer_dims(stride):
    hc = _round_up((CONV_W - 1) * stride, SUBLANES)
    if stride % SUBLANES == 0:
        starts = tuple((w - 1) * stride for w in POOL_WINDOWS)
        hp = (POOL_MAX - 1) * stride
    else:
        starts = (SUBLANES,) * 4
        hp = _round_up(POOL_MAX - 1, SUBLANES) + SUBLANES
    return hc, hp, starts


def _mixers_kernel(*refs, stride, tb, start_pos, prompt):
    (x_ref, wabc_ref, wqkv_ref, widx_ref, wkt_ref, convw_ref, convb_ref, wgate_ref, bgate_ref, lam_ref,
     lng_ref, lnb_ref, ws_ref, bs_ref, poolw_ref, pscale_ref, conv0_ref, pool0_ref, h0_ref) = refs[:19]
    outs = refs[19:]
    if prompt:
        (yabc_ref, q_ref, k_ref, v_ref, qi_ref, kiwi_ref, ctail_ref, ptail_ref, hlast_ref,
         kt_ref, kit_ref, vbd_ref, cext, pext, l1, l2, l3, hcar) = outs
    else:
        (yabc_ref, q_ref, k_ref, v_ref, qi_ref, kiwi_ref, ctail_ref, ptail_ref, hlast_ref,
         vn_ref, cext, pext, l1, l2, l3, hcar) = outs
    hc, hp, starts = _mixer_dims(stride)
    tstep = pl.program_id(1)
    gw = GROUP_W

    @pl.when(tstep == 0)
    def _():
        cext[0:hc, :] = conv0_ref[0]
        pext[0:hp, :] = pool0_ref[0]
        hcar[...] = h0_ref[0]

    xb = x_ref[0].astype(BF16)
    za = _dot(xb, wabc_ref[...])
    zq = _dot(xb, wqkv_ref[...])
    zi = _dot(xb, widx_ref[...])
    q_ref[0] = zq[:, 0:gw].astype(BF16)
    k_ref[0] = zq[:, gw:2 * gw]
    vv = zq[:, 2 * gw:3 * gw]
    v_ref[0] = vv
    qi_ref[0] = zi[:, 0:IDX_HEADS * IDX_DIM].astype(BF16)
    kiwi_ref[0] = zi[:, IDX_HEADS * IDX_DIM:]
    if prompt:
        zt = _dot_nt(wkt_ref[...], xb)
        for c in range(tb // KEY_CHUNK):
            cols = slice(c * KEY_CHUNK, (c + 1) * KEY_CHUNK)
            kt_ref[0, c] = zt[0:gw, cols].astype(BF16)
            kit_ref[0, c] = zt[gw:gw + IDX_DIM, cols].astype(BF16)
        vhead = _lane_head((tb, gw), HEAD_DIM)
        for h in range(ATT_HEADS):
            vbd_ref[0, h] = jnp.where(vhead == h, vv, 0.0).astype(BF16)

    xa = za[:, 0:gw]
    ga = za[:, gw:2 * gw]
    cext[hc:hc + tb, :] = xa
    xc = convb_ref[...] + jnp.zeros((tb, gw), F32)
    for j in range(CONV_W):
        off = hc + (j - (CONV_W - 1)) * stride
        xc = xc + cext[off:off + tb, :] * convw_ref[j:j + 1, :]
    ctail = cext[tb:tb + hc, :]
    ctail_ref[0] = ctail
    cext[0:hc, :] = ctail
    gates = _dot(xc.astype(BF16), wgate_ref[...]) + bgate_ref[...]
    r = jax.nn.sigmoid(gates[:, 0:gw])
    ig = jax.nn.sigmoid(gates[:, gw:2 * gw])
    nl = -lam_ref[...]
    softplus = jnp.maximum(nl, 0.0) + jnp.log1p(jnp.exp(-jnp.abs(nl)))
    log_a = -RG_C * r * softplus
    a = jnp.exp(log_a)
    mult = jnp.sqrt(-jnp.tanh(log_a) * (a * a + 1.0))
    bv = mult * ig * xc
    rows = lax.broadcasted_iota(I32, (tb, gw), 0)
    av = a
    d = stride
    while d < tb:
        a_sh = jnp.where(rows >= d, pltpu.roll(av, d, 0), 1.0)
        b_sh = jnp.where(rows >= d, pltpu.roll(bv, d, 0), 0.0)
        bv = av * b_sh + bv
        av = av * a_sh
        d *= 2
    hprev = hcar[...]
    if stride == 1:
        hfull = av * hprev + bv
    else:
        hfull = av * jnp.concatenate([hprev] * (tb // stride), axis=0) + bv
    hl = hfull[tb - stride:tb, :]
    hcar[...] = hl
    hlast_ref[0] = hl
    yabc_ref[0, :, 0:gw] = (hfull * _gelu(ga)).astype(BF16)

    u = _gelu(za[:, 2 * gw:3 * gw])
    vn = _layer_norm(_gelu(za[:, 3 * gw:4 * gw]), lng_ref[...], lnb_ref[...])
    if not prompt:
        vn_ref[0] = vn
    ghead = _lane_head((CHUNK, gw), gw // G_HEADS)
    for c in range(tb // CHUNK):
        crow = slice(c * CHUNK, (c + 1) * CHUNK)
        vnc = vn[crow].astype(BF16)
        mixed = bs_ref[...]
        for h in range(G_HEADS):
            mixed = mixed + jnp.where(ghead == h, _dot(ws_ref[h], vnc), 0.0)
        yabc_ref[0, crow, gw:2 * gw] = (u[crow] * mixed).astype(BF16)

    xp = za[:, 4 * gw:5 * gw]
    n0 = hp + tb
    pext[hp:n0, :] = xp
    s1, s2, s3, s4 = starts
    zero8 = jnp.zeros((max(s3, SUBLANES), gw), F32)
    l1[0:s1, :] = zero8[0:s1]
    l2[0:s2, :] = zero8[0:s2]
    l3[0:s3, :] = zero8[0:s3]
    l1[s1:n0, :] = pext[s1:n0, :] + pext[s1 - stride:n0 - stride, :]
    l2[s2:n0, :] = l1[s2:n0, :] + l1[s2 - 2 * stride:n0 - 2 * stride, :]
    l3[s3:n0, :] = l2[s3:n0, :] + l2[s3 - 4 * stride:n0 - 4 * stride, :]
    w16 = l3[hp:n0, :] + l3[hp - 8 * stride:n0 - 8 * stride, :]
    pgrp = _lane_head((tb, gw), gw // len(POOL_WINDOWS))
    sums = jnp.where(pgrp == 0, l1[hp:n0, :], jnp.where(pgrp == 1, l2[hp:n0, :],
                     jnp.where(pgrp == 2, l3[hp:n0, :], w16)))
    win = jnp.where(pgrp == 0, POOL_WINDOWS[0], jnp.where(pgrp == 1, POOL_WINDOWS[1],
                    jnp.where(pgrp == 2, POOL_WINDOWS[2], POOL_WINDOWS[3])))
    pos = start_pos + _div_pow2(tstep * tb + rows, stride)
    cnt = jnp.minimum(pos + 1, win).astype(F32)
    pooled = sums / cnt - xp
    yc = _dot(pooled.astype(BF16), poolw_ref[...]) * pscale_ref[...]
    yabc_ref[0, :, 2 * gw:3 * gw] = yc.astype(BF16)
    ptail = pext[n0 - hp:n0, :]
    ptail_ref[0] = ptail
    if tb >= hp:
        pext[0:hp, :] = ptail


def _block_diag(w):
    h, a, b = w.shape
    eye = jnp.eye(h, dtype=w.dtype)
    return (eye[:, None, :, None] * w[:, :, None, :]).reshape(h * a, h * b)


def _mixers(x, lw, conv0, pool0, h0, *, stride, tb, start_pos, prompt):
    nseq, rows, dm = x.shape
    gw = GROUP_W
    hc, hp, starts = _mixer_dims(stride)
    nt = rows // tb
    n0 = hp + tb
    kern = functools.partial(_mixers_kernel, stride=stride, tb=tb, start_pos=start_pos, prompt=prompt)

    def const(shape):
        return pl.BlockSpec(shape, lambda b, t: (0,) * len(shape))

    def per_seq(shape):
        return pl.BlockSpec((1,) + shape, lambda b, t: (b,) + (0,) * len(shape))

    def per_blk(width):
        return pl.BlockSpec((1, tb, width), lambda b, t: (b, t, 0))

    in_specs = [per_blk(dm), const(lw["w_abc"].shape), const(lw["w_qkv"].shape), const(lw["w_idx"].shape),
                const(lw["w_kt"].shape), const((SUBLANES, gw)), const((1, gw)), const((gw, 2 * gw)),
                const((1, 2 * gw)), const((1, gw)), const((1, gw)), const((1, gw)),
                const((G_HEADS, CHUNK, CHUNK)), const((CHUNK, gw)), const((gw, gw)), const((1, gw)),
                per_seq((hc, gw)), per_seq((hp, gw)), per_seq((stride, gw))]
    out_shape = [jax.ShapeDtypeStruct((nseq, rows, 3 * gw), BF16),
                 jax.ShapeDtypeStruct((nseq, rows, gw), BF16),
                 jax.ShapeDtypeStruct((nseq, rows, gw), F32),
                 jax.ShapeDtypeStruct((nseq, rows, gw), F32),
                 jax.ShapeDtypeStruct((nseq, rows, IDX_HEADS * IDX_DIM), BF16),
                 jax.ShapeDtypeStruct((nseq, rows, LANES), F32),
                 jax.ShapeDtypeStruct((nseq, hc, gw), F32),
                 jax.ShapeDtypeStruct((nseq, hp, gw), F32),
                 jax.ShapeDtypeStruct((nseq, stride, gw), F32)]
    out_specs = [per_blk(3 * gw), per_blk(gw), per_blk(gw), per_blk(gw), per_blk(IDX_HEADS * IDX_DIM),
                 per_blk(LANES), per_seq((hc, gw)), per_seq((hp, gw)), per_seq((stride, gw))]
    if prompt:
        nck = rows // KEY_CHUNK
        cpb = tb // KEY_CHUNK
        out_shape += [jax.ShapeDtypeStruct((nseq, nck, gw, KEY_CHUNK), BF16),
                      jax.ShapeDtypeStruct((nseq, nck, IDX_DIM, KEY_CHUNK), BF16),
                      jax.ShapeDtypeStruct((nseq, ATT_HEADS, rows, gw), BF16)]
        out_specs += [pl.BlockSpec((1, cpb, gw, KEY_CHUNK), lambda b, t: (b, t, 0, 0)),
                      pl.BlockSpec((1, cpb, IDX_DIM, KEY_CHUNK), lambda b, t: (b, t, 0, 0)),
                      pl.BlockSpec((1, ATT_HEADS, tb, gw), lambda b, t: (b, 0, t, 0))]
    else:
        out_shape += [jax.ShapeDtypeStruct((nseq, rows, gw), F32)]
        out_specs += [per_blk(gw)]
    scratch = [pltpu.VMEM((hc + tb, gw), F32), pltpu.VMEM((n0, gw), F32), pltpu.VMEM((n0, gw), F32),
               pltpu.VMEM((n0, gw), F32), pltpu.VMEM((n0, gw), F32), pltpu.VMEM((stride, gw), F32)]
    return pl.pallas_call(
        kern, out_shape=out_shape, grid=(nseq, nt), in_specs=in_specs, out_specs=out_specs,
        scratch_shapes=scratch, name="mixers_prompt" if prompt else "mixers_sample",
        compiler_params=pltpu.CompilerParams(dimension_semantics=("arbitrary", "arbitrary"),
                                             vmem_limit_bytes=VMEM_LIMIT),
    )(x, lw["w_abc"], lw["w_qkv"], lw["w_idx"], lw["w_kt"], lw["conv_w"], lw["conv_b"], lw["w_gate"],
      lw["b_gate"], lw["lam"], lw["ln_g"], lw["ln_b"], lw["ws_p" if prompt else "ws_s"],
      lw["bs_p" if prompt else "bs_s"], lw["pool_w"], lw["pool_scale"], conv0, pool0, h0)


def _count_rows(acc):
    return jnp.sum(acc, axis=1, keepdims=True)


def _kth_largest_key(count_ge, shape, topk):
    def step(it, base):
        bit = lax.shift_left(jnp.int32(1), 31 - it)
        cand = jnp.where(it == 0, jnp.int32(0), base | bit)
        return jnp.where(count_ge(cand) >= topk, cand, base)
    return lax.fori_loop(0, 32, step, jnp.full(shape, INT_MIN, I32))


def _tie_aware_select(key, thr, need, carry, tri):
    eq = (key == thr) & (key != INT_MIN)
    pref = _dot(jnp.where(eq, 1.0, 0.0).astype(BF16), tri) + carry
    sel = (key > thr) | (eq & (pref <= need))
    new_carry = jnp.broadcast_to(pref[:, KEY_CHUNK - 1:KEY_CHUNK], pref.shape)
    return sel, new_carry


def _upper_tri_ones():
    r = lax.broadcasted_iota(I32, (KEY_CHUNK, KEY_CHUNK), 0)
    c = lax.broadcasted_iota(I32, (KEY_CHUNK, KEY_CHUNK), 1)
    return jnp.where(r <= c, 1.0, 0.0).astype(BF16)


def _dsa_prompt_kernel(b31_ref, qi_ref, kiwi_ref, q_ref, kit_ref, kt_ref, vbd_ref, btab_ref, out_ref,
                       qh_scr, wb_scr, qs_scr, s_scr, m_scr, l_scr, acc_scr, *, topk, qb):
    i = pl.program_id(1)
    gw = GROUP_W

    qi = qi_ref[0]
    kiwi = kiwi_ref[0]
    for h in range(IDX_HEADS):
        qh_scr[h * qb:(h + 1) * qb, :] = qi[:, h * IDX_DIM:(h + 1) * IDX_DIM]
        wcol = kiwi[:, IDX_DIM + h:IDX_DIM + h + 1] * IDX_SCALE
        wb_scr[h] = jnp.broadcast_to(wcol, (qb, LANES))
    qq = q_ref[0]
    for h in range(ATT_HEADS):
        qs_scr[h] = qq[:, h * HEAD_DIM:(h + 1) * HEAD_DIM]

    row = lax.broadcasted_iota(I32, (qb, KEY_CHUNK), 0)
    col = lax.broadcasted_iota(I32, (qb, KEY_CHUNK), 1)

    def score_chunk(j, causal):
        rr = _dot(qh_scr[...], kit_ref[0, j])
        s = jnp.zeros((qb, KEY_CHUNK), F32)
        for h in range(IDX_HEADS):
            s = s + wb_scr[h] * jnp.maximum(rr[h * qb:(h + 1) * qb], 0.0)
        key = _score_key(s)
        if causal:
            key = jnp.where(col <= row, key, INT_MIN)
        s_scr[j] = key

    def score_body(j, c):
        score_chunk(j, False)
        return c
    lax.fori_loop(0, i, score_body, 0)
    score_chunk(i, True)

    def count(pred):
        def body(j, acc):
            return acc + jnp.where(pred(s_scr[j]), 1.0, 0.0)
        return _count_rows(lax.fori_loop(0, i + 1, body, jnp.zeros((qb, KEY_CHUNK), F32)))

    thr = _kth_largest_key(lambda cand: count(lambda k: k >= cand), (qb, KEY_CHUNK), topk)
    need = jnp.broadcast_to(topk - count(lambda k: k > thr), (qb, KEY_CHUNK))

    m_scr[...] = jnp.full(m_scr.shape, NEG_BIG, F32)
    l_scr[...] = jnp.zeros(l_scr.shape, F32)
    acc_scr[...] = jnp.zeros(acc_scr.shape, F32)
    tri = _upper_tri_ones()
    lane64 = lax.broadcasted_iota(I32, (qb, LANES), 1) < HEAD_DIM

    def att_body(j, carry):
        sel, carry = _tie_aware_select(s_scr[j], thr, need, carry, tri)
        near = j >= i - 1
        tidx = jnp.clip(j - (i - 1), 0, 1)
        pv = jnp.zeros((qb, gw), F32)
        alphas = []
        for h in range(ATT_HEADS):
            lg = _dot(qs_scr[h], kt_ref[0, j, pl.ds(h * HEAD_DIM, HEAD_DIM), :]) * ATT_SCALE
            bias = jnp.where(near, btab_ref[h, tidx], b31_ref[h])
            lg = jnp.where(sel, lg + bias, NEG_BIG)
            m_old = m_scr[h]
            m_new = jnp.maximum(m_old, jnp.max(lg, axis=1, keepdims=True))
            alpha = jnp.exp(m_old - m_new)
            p = jnp.where(sel, jnp.exp(lg - m_new), 0.0)
            l_scr[h] = alpha * l_scr[h] + jnp.sum(p, axis=1, keepdims=True)
            m_scr[h] = m_new
            vrows = pl.ds(pl.multiple_of(j * KEY_CHUNK, KEY_CHUNK), KEY_CHUNK)
            pv = pv + _dot(p.astype(BF16), vbd_ref[0, h, vrows, :])
            alphas.append(alpha)
        alpha_cat = jnp.concatenate([jnp.where(lane64, alphas[0], alphas[1]),
                                     jnp.where(lane64, alphas[2], alphas[3])], axis=1)
        acc_scr[...] = acc_scr[...] * alpha_cat + pv
        return carry

    lax.fori_loop(0, i + 1, att_body, jnp.zeros((qb, KEY_CHUNK), F32))
    l_cat = jnp.concatenate([jnp.where(lane64, l_scr[0], l_scr[1]),
                             jnp.where(lane64, l_scr[2], l_scr[3])], axis=1)
    out_ref[0] = (acc_scr[...] / l_cat).astype(BF16)


def _dsa_prompt(qi, kiwi, q, kit, kt, vbd, btab, b31, *, topk):
    nb, t, _ = q.shape
    qb = KEY_CHUNK
    nq = t // qb
    gw = GROUP_W
    kern = functools.partial(_dsa_prompt_kernel, topk=topk, qb=qb)

    def per_q(width):
        return pl.BlockSpec((1, qb, width), lambda b, i: (b, i, 0))

    def per_batch(shape):
        return pl.BlockSpec((1,) + shape, lambda b, i: (b,) + (0,) * len(shape))

    in_specs = [pl.BlockSpec(memory_space=pltpu.SMEM),
                per_q(IDX_HEADS * IDX_DIM), per_q(LANES), per_q(gw),
                per_batch((nq, IDX_DIM, KEY_CHUNK)), per_batch((nq, gw, KEY_CHUNK)),
                pl.BlockSpec((1, ATT_HEADS, t, gw), lambda b, i: (b, 0, 0, 0), pipeline_mode=pl.Buffered(1)),
                pl.BlockSpec((ATT_HEADS, 2, KEY_CHUNK, KEY_CHUNK), lambda b, i: (0, 0, 0, 0))]
    scratch = [pltpu.VMEM((IDX_HEADS * qb, IDX_DIM), BF16), pltpu.VMEM((IDX_HEADS, qb, LANES), F32),
               pltpu.VMEM((ATT_HEADS, qb, HEAD_DIM), BF16), pltpu.VMEM((nq, qb, KEY_CHUNK), I32),
               pltpu.VMEM((ATT_HEADS, qb, LANES), F32), pltpu.VMEM((ATT_HEADS, qb, LANES), F32),
               pltpu.VMEM((qb, gw), F32)]
    return pl.pallas_call(
        kern, out_shape=jax.ShapeDtypeStruct((nb, t, gw), BF16), grid=(nb, nq),
        in_specs=in_specs, out_specs=per_q(gw), scratch_shapes=scratch, name="dsa_prompt",
        compiler_params=pltpu.CompilerParams(dimension_semantics=("arbitrary", "arbitrary"),
                                             vmem_limit_bytes=VMEM_LIMIT),
    )(b31, qi, kiwi, q, kit, kt, vbd, btab)


PAGES_PER_STEP = 8


def _sample_scores(qi, wpad, keys_bf16):
    rr = _dot_nt(qi, keys_bf16)
    s = jnp.zeros((SUBLANES, rr.shape[1]), F32)
    for h in range(IDX_HEADS):
        rows = slice(h * SUBLANES, (h + 1) * SUBLANES)
        s = s + wpad[rows] * jnp.maximum(rr[rows], 0.0)
    return _score_key(s)


def _dsa_sample_index_kernel(pt_ref, qi_ref, wpad_ref, *refs, npages, topk):
    g = PAGES_PER_STEP
    page_refs = refs[:g]
    kinew_ref, spast_ref, snew_ref, thr_ref, need_ref = refs[g:]
    s = pl.program_id(1)
    qi = qi_ref[0]
    wpad = wpad_ref[0]
    for p in range(g):
        spast_ref[0, s * g + p] = _sample_scores(qi, wpad, page_refs[p][0, 0].astype(BF16))

    @pl.when(s == pl.num_programs(1) - 1)
    def _():
        row = lax.broadcasted_iota(I32, (SUBLANES, KEY_CHUNK), 0)
        col = lax.broadcasted_iota(I32, (SUBLANES, KEY_CHUNK), 1)
        knew = jnp.where(col <= row, _sample_scores(qi, wpad, kinew_ref[0]), INT_MIN)
        snew_ref[0] = knew

        def count(pred):
            def body(j, acc):
                return acc + jnp.where(pred(spast_ref[0, j]), 1.0, 0.0)
            acc = lax.fori_loop(0, npages, body, jnp.where(pred(knew), 1.0, 0.0))
            return _count_rows(acc)

        thr = _kth_largest_key(lambda cand: count(lambda k: k >= cand), (SUBLANES, KEY_CHUNK), topk)
        thr_ref[0] = thr
        need_ref[0] = jnp.broadcast_to(topk - count(lambda k: k > thr), (SUBLANES, KEY_CHUNK))


def _dsa_sample_attend_kernel(pt_ref, qbd_ref, spast_ref, snew_ref, thr_ref, need_ref, *refs, npages):
    g = PAGES_PER_STEP
    k_refs = refs[:g]
    v_refs = refs[g:2 * g]
    knew_ref, vnew_ref, tabs_ref, out_ref, m_scr, l_scr, acc_scr, carry_scr = refs[2 * g:]
    s = pl.program_id(1)
    nh = ATT_HEADS * SUBLANES

    @pl.when(s == 0)
    def _():
        m_scr[...] = jnp.full(m_scr.shape, NEG_BIG, F32)
        l_scr[...] = jnp.zeros(l_scr.shape, F32)
        acc_scr[...] = jnp.zeros(acc_scr.shape, F32)
        carry_scr[...] = jnp.zeros(carry_scr.shape, F32)

    qbd = qbd_ref[0]
    thr = thr_ref[0]
    need = need_ref[0]
    tri = _upper_tri_ones()

    def attend(key, kb, vb, bias):
        sel8, carry = _tie_aware_select(key, thr, need, carry_scr[...], tri)
        carry_scr[...] = carry
        sel = jnp.concatenate([jnp.where(sel8, 1, 0)] * ATT_HEADS, axis=0) > 0
        lg = jnp.where(sel, _dot_nt(qbd, kb) * ATT_SCALE + bias, NEG_BIG)
        m_old = m_scr[...]
        m_new = jnp.maximum(m_old, jnp.max(lg, axis=1, keepdims=True))
        alpha = jnp.exp(m_old - m_new)
        p = jnp.where(sel, jnp.exp(lg - m_new), 0.0)
        l_scr[...] = alpha * l_scr[...] + jnp.sum(p, axis=1, keepdims=True)
        m_scr[...] = m_new
        alpha2 = jnp.concatenate([alpha, alpha], axis=1)
        acc_scr[...] = acc_scr[...] * alpha2 + _dot(p.astype(BF16), vb)

    for p in range(g):
        page = s * g + p
        bias = jnp.where(page == npages - 1, tabs_ref[0], tabs_ref[2])
        attend(spast_ref[0, p], k_refs[p][0, 0].astype(BF16), v_refs[p][0, 0].astype(BF16), bias)

    @pl.when(s == pl.num_programs(1) - 1)
    def _():
        attend(snew_ref[0], knew_ref[0], vnew_ref[0], tabs_ref[1])
        l2 = jnp.concatenate([l_scr[...], l_scr[...]], axis=1)
        o = acc_scr[...] / l2
        head = _lane_head((SUBLANES, GROUP_W), HEAD_DIM)
        out = jnp.zeros((SUBLANES, GROUP_W), F32)
        for h in range(ATT_HEADS):
            out = out + jnp.where(head == h, o[h * SUBLANES:(h + 1) * SUBLANES], 0.0)
        out_ref[0] = out


def _dsa_sample(layer, page_table, qi_pad, wpad, kinew, qbd, knew, vnew, cache_ik, cache_k, cache_v, tabs, *, topk):
    db, npages = page_table.shape
    g = PAGES_PER_STEP
    ns = npages // g
    gw = GROUP_W
    pt = page_table.reshape(-1)

    def page_spec(width, p):
        return pl.BlockSpec((1, 1, PAGE_SIZE, width),
                            lambda b, s, pt_ref: (layer, pt_ref[b * npages + s * g + p], 0, 0))

    def per_b(shape):
        return pl.BlockSpec((1,) + shape, lambda b, s, pt_ref: (b,) + (0,) * len(shape))

    index_kern = functools.partial(_dsa_sample_index_kernel, npages=npages, topk=topk)
    spast, snew, thr, need = pl.pallas_call(
        index_kern,
        out_shape=(jax.ShapeDtypeStruct((db, npages, SUBLANES, KEY_CHUNK), I32),
                   jax.ShapeDtypeStruct((db, SUBLANES, KEY_CHUNK), I32),
                   jax.ShapeDtypeStruct((db, SUBLANES, KEY_CHUNK), I32),
                   jax.ShapeDtypeStruct((db, SUBLANES, KEY_CHUNK), F32)),
        grid_spec=pltpu.PrefetchScalarGridSpec(
            num_scalar_prefetch=1, grid=(db, ns),
            in_specs=[per_b((IDX_HEADS * SUBLANES, IDX_DIM)), per_b((IDX_HEADS * SUBLANES, LANES))]
                     + [page_spec(IDX_DIM, p) for p in range(g)] + [per_b((KEY_CHUNK, IDX_DIM))],
            out_specs=[per_b((npages, SUBLANES, KEY_CHUNK)), per_b((SUBLANES, KEY_CHUNK)),
                       per_b((SUBLANES, KEY_CHUNK)), per_b((SUBLANES, KEY_CHUNK))]),
        name="dsa_sample_index",
        compiler_params=pltpu.CompilerParams(dimension_semantics=("arbitrary", "arbitrary"),
                                             vmem_limit_bytes=VMEM_LIMIT),
    )(pt, qi_pad, wpad, *([cache_ik] * g), kinew)

    attend_kern = functools.partial(_dsa_sample_attend_kernel, npages=npages)
    nh = ATT_HEADS * SUBLANES
    return pl.pallas_call(
        attend_kern,
        out_shape=jax.ShapeDtypeStruct((db, SUBLANES, gw), F32),
        grid_spec=pltpu.PrefetchScalarGridSpec(
            num_scalar_prefetch=1, grid=(db, ns),
            in_specs=[per_b((nh, gw)),
                      pl.BlockSpec((1, g, SUBLANES, KEY_CHUNK), lambda b, s, pt_ref: (b, s, 0, 0)),
                      per_b((SUBLANES, KEY_CHUNK)), per_b((SUBLANES, KEY_CHUNK)), per_b((SUBLANES, KEY_CHUNK))]
                     + [page_spec(gw, p) for p in range(g)] + [page_spec(gw, p) for p in range(g)]
                     + [per_b((KEY_CHUNK, gw)), per_b((KEY_CHUNK, gw)),
                        pl.BlockSpec((3, nh, KEY_CHUNK), lambda b, s, pt_ref: (0, 0, 0))],
            out_specs=per_b((SUBLANES, gw)),
            scratch_shapes=[pltpu.VMEM((nh, LANES), F32), pltpu.VMEM((nh, LANES), F32),
                            pltpu.VMEM((nh, gw), F32), pltpu.VMEM((SUBLANES, KEY_CHUNK), F32)]),
        name="dsa_sample_attend",
        compiler_params=pltpu.CompilerParams(dimension_semantics=("arbitrary", "arbitrary"),
                                             vmem_limit_bytes=VMEM_LIMIT),
    )(pt, qbd, spast, snew, thr, need, *([cache_k] * g), *([cache_v] * g), knew, vnew, tabs)


FF_CHUNK = 1024


def _ffn_kernel(x_ref, yabc_ref, yd_ref, woa_ref, wod_ref, g1_ref, b1_ref, w1_ref, w2_ref, g2_ref, b2_ref,
                out_ref, *, alpha):
    x = x_ref[...]
    mix = _dot(yabc_ref[...], woa_ref[...]) + _dot(yd_ref[...], wod_ref[...])
    x1 = _layer_norm(alpha * x + mix, g1_ref[...], b1_ref[...])
    x1b = x1.astype(BF16)
    acc = jnp.zeros(x.shape, F32)
    for c in range(w1_ref.shape[1] // FF_CHUNK):
        cols = slice(c * FF_CHUNK, (c + 1) * FF_CHUNK)
        hdn = jnp.square(jnp.maximum(_dot(x1b, w1_ref[:, cols]), 0.0))
        acc = acc + _dot(hdn.astype(BF16), w2_ref[cols, :])
    out_ref[...] = _layer_norm(alpha * x1 + acc, g2_ref[...], b2_ref[...])


def _ffn(x, yabc, yd, lw, *, tm, alpha):
    rows, dm = x.shape
    dff = lw["w1"].shape[1]

    def blk(width):
        return pl.BlockSpec((tm, width), lambda r: (r, 0))

    def const(shape):
        return pl.BlockSpec(shape, lambda r: (0,) * len(shape), pipeline_mode=pl.Buffered(1))

    return pl.pallas_call(
        functools.partial(_ffn_kernel, alpha=alpha),
        out_shape=jax.ShapeDtypeStruct((rows, dm), F32), grid=(rows // tm,),
        in_specs=[blk(dm), blk(3 * GROUP_W), blk(GROUP_W), const((3 * GROUP_W, dm)), const((GROUP_W, dm)),
                  const((1, dm)), const((1, dm)), const((dm, dff)), const((dff, dm)), const((1, dm)),
                  const((1, dm))],
        out_specs=blk(dm), name="ffn",
        compiler_params=pltpu.CompilerParams(dimension_semantics=("arbitrary",),
                                             vmem_limit_bytes=VMEM_LIMIT),
    )(x, yabc, yd, lw["w_out_abc"], lw["w_out_d"], lw["ln1_g"], lw["ln1_b"], lw["w1"], lw["w2"],
      lw["ln2_g"], lw["ln2_b"])


def _layer_weights(l, dec_batch, dec_seq, w_in, conv_w, conv_b, lru_wa, lru_ba, lru_wx, lru_bx, lru_lambda,
                   gmlp_ln_g, gmlp_ln_b, gmlp_ws, gmlp_bs, pool_w, pool_scale, w_out, ln1_g, ln1_b, ln2_g,
                   ln2_b, ffn_w1, ffn_w2):
    gw = GROUP_W
    dm = w_in.shape[1]
    wi = w_in[l]
    o_q, o_k, o_v = 5 * gw, 6 * gw, 7 * gw
    o_qi = 8 * gw
    o_ki = o_qi + IDX_HEADS * IDX_DIM
    o_wi = o_ki + IDX_DIM
    pad = jnp.zeros((dm, LANES - IDX_DIM - IDX_HEADS), wi.dtype)
    row = lambda a: a[l].reshape(1, -1)
    tril = jnp.tril(jnp.ones((CHUNK, CHUNK), F32))
    ws_tril = gmlp_ws[l] * tril
    eye_b = jnp.eye(dec_batch, dtype=F32)
    ws_small = ws_tril[:, :dec_seq, :dec_seq]
    ws_s = (ws_small[:, :, None, :, None] * eye_b[None, None, :, None, :]).reshape(
        G_HEADS, dec_seq * dec_batch, dec_seq * dec_batch)
    bs_cols = jnp.repeat(gmlp_bs[l].T, gw // G_HEADS, axis=1)
    return {
        "w_abc": wi[:, :o_q].astype(BF16),
        "w_qkv": wi[:, o_q:o_qi].astype(BF16),
        "w_idx": jnp.concatenate([wi[:, o_qi:], pad], axis=1).astype(BF16),
        "w_kt": jnp.concatenate([wi[:, o_k:o_v], wi[:, o_ki:o_wi]], axis=1).T.astype(BF16),
        "conv_w": jnp.concatenate([conv_w[l], jnp.zeros((SUBLANES - CONV_W, gw), F32)], axis=0),
        "conv_b": row(conv_b),
        "w_gate": jnp.concatenate([_block_diag(lru_wa[l]), _block_diag(lru_wx[l])], axis=1).astype(BF16),
        "b_gate": jnp.concatenate([lru_ba[l], lru_bx[l]]).reshape(1, -1),
        "lam": row(lru_lambda), "ln_g": row(gmlp_ln_g), "ln_b": row(gmlp_ln_b),
        "ws_p": ws_tril.astype(BF16), "ws_s": ws_s.astype(BF16),
        "bs_p": bs_cols, "bs_s": jnp.repeat(bs_cols[:dec_seq], dec_batch, axis=0),
        "pool_w": _block_diag(pool_w[l]).astype(BF16), "pool_scale": row(pool_scale),
        "w_out_abc": w_out[l, :3 * gw].astype(BF16), "w_out_d": w_out[l, 3 * gw:].astype(BF16),
        "ln1_g": row(ln1_g), "ln1_b": row(ln1_b), "ln2_g": row(ln2_g), "ln2_b": row(ln2_b),
        "w1": ffn_w1[l].astype(BF16), "w2": ffn_w2[l].astype(BF16),
    }


PROMPT_TB = 256
FFN_TM = 512


def kernel(x_prompt, x_sample, cache_k, cache_v, cache_idx_k, state_lru_h, state_conv, state_pool, page_table,
           w_in, conv_w, conv_b, lru_wa, lru_ba, lru_wx, lru_bx, lru_lambda, gmlp_ln_g, gmlp_ln_b, gmlp_ws,
           gmlp_bs, pool_w, pool_scale, rel_bias, w_out, ln1_g, ln1_b, ln2_g, ln2_b, ffn_w1, ffn_w2):
    nb, t, dm = x_prompt.shape
    db, ts, _ = x_sample.shape
    depth = w_in.shape[0]
    gw = GROUP_W
    npages = page_table.shape[1]
    past_len = npages * PAGE_SIZE
    topk_p = min(TOPK_MAX, t // 4)
    topk_s = min(TOPK_MAX, (past_len + ts) // 4)
    alpha = (2 * depth) ** 0.25
    n_pool = cache_k.shape[1]
    cache_k2 = cache_k.reshape(depth, n_pool, PAGE_SIZE, gw)
    cache_v2 = cache_v.reshape(depth, n_pool, PAGE_SIZE, gw)

    btab, tabs = _bias_tables(rel_bias)
    b31 = rel_bias[NUM_BUCKETS - 1]
    hc_p, hp_p, _ = _mixer_dims(1)
    hc_s, hp_s, _ = _mixer_dims(db)
    srows = ts * db

    def to_rows(a):
        return jnp.swapaxes(a, 0, 1).reshape((srows,) + a.shape[2:])

    def from_rows(a):
        return jnp.swapaxes(a.reshape((ts, db) + a.shape[1:]), 0, 1)

    xp = x_prompt
    xs = to_rows(x_sample)[None]
    st_p, st_s = [], []
    for l in range(depth):
        lw = _layer_weights(l, db, ts, w_in, conv_w, conv_b, lru_wa, lru_ba, lru_wx, lru_bx, lru_lambda,
                            gmlp_ln_g, gmlp_ln_b, gmlp_ws, gmlp_bs, pool_w, pool_scale, w_out, ln1_g, ln1_b,
                            ln2_g, ln2_b, ffn_w1, ffn_w2)
        (yabc, q, k, v, qi, kiwi, ctail, ptail, hlast, kt, kit, vbd) = _mixers(
            xp, lw, jnp.zeros((nb, hc_p, gw), F32), jnp.zeros((nb, hp_p, gw), F32), jnp.zeros((nb, 1, gw), F32),
            stride=1, tb=PROMPT_TB, start_pos=0, prompt=True)
        yd = _dsa_prompt(qi, kiwi, q, kit, kt, vbd, btab, b31, topk=topk_p)
        xp = _ffn(xp.reshape(nb * t, dm), yabc.reshape(nb * t, 3 * gw), yd.reshape(nb * t, gw), lw,
                  tm=FFN_TM, alpha=alpha).reshape(nb, t, dm)
        st_p.append((k.reshape(nb, t, ATT_HEADS, HEAD_DIM), v.reshape(nb, t, ATT_HEADS, HEAD_DIM),
                     kiwi[:, :, :IDX_DIM], hlast[:, 0], ctail[:, hc_p - (CONV_W - 1):],
                     ptail[:, hp_p - (POOL_MAX - 1):]))
        conv0 = jnp.swapaxes(state_conv[l], 0, 1).reshape(1, hc_s, gw)
        pool0 = jnp.swapaxes(state_pool[l], 0, 1).reshape(1, hp_s, gw)
        (yabc_s, q_s, k_s, v_s, qi_s, kiwi_s, ctail_s, ptail_s, hlast_s, vn_s) = _mixers(
            xs, lw, conv0, pool0, state_lru_h[l][None], stride=db, tb=srows, start_pos=past_len, prompt=False)
        k_new = from_rows(k_s[0])
        v_new = from_rows(v_s[0])
        ki_new = from_rows(kiwi_s[0, :, :IDX_DIM])
        w_new = from_rows(kiwi_s[0, :, IDX_DIM:IDX_DIM + IDX_HEADS])
        qi_new = from_rows(qi_s[0]).reshape(db, ts, IDX_HEADS, IDX_DIM)
        q_new = from_rows(q_s[0]).reshape(db, ts, ATT_HEADS, HEAD_DIM)
        tpad = SUBLANES - ts
        qi_pad = jnp.pad(jnp.swapaxes(qi_new, 1, 2), ((0, 0), (0, 0), (0, tpad), (0, 0))).reshape(
            db, IDX_HEADS * SUBLANES, IDX_DIM)
        wpad = jnp.pad(jnp.swapaxes(w_new, 1, 2) * IDX_SCALE, ((0, 0), (0, 0), (0, tpad))).reshape(
            db, IDX_HEADS * SUBLANES, 1)
        wpad = jnp.broadcast_to(wpad, (db, IDX_HEADS * SUBLANES, LANES))
        q_heads = jnp.pad(jnp.swapaxes(q_new, 1, 2), ((0, 0), (0, 0), (0, tpad), (0, 0)))
        eye_h = jnp.eye(ATT_HEADS, dtype=q_heads.dtype)
        qbd = (q_heads[:, :, :, None, :] * eye_h[None, :, None, :, None]).reshape(
            db, ATT_HEADS * SUBLANES, gw)
        rpad = ((0, 0), (0, KEY_CHUNK - ts), (0, 0))
        yd_s = _dsa_sample(l, page_table, qi_pad, wpad, jnp.pad(ki_new, rpad).astype(BF16), qbd,
                           jnp.pad(k_new, rpad).astype(BF16), jnp.pad(v_new, rpad).astype(BF16),
                           cache_idx_k, cache_k2, cache_v2, tabs, topk=topk_s)
        yd_rows = to_rows(yd_s[:, :ts]).astype(BF16)
        xs = _ffn(xs[0], yabc_s[0], yd_rows, lw, tm=srows, alpha=alpha)[None]
        st_s.append((k_new.reshape(db, ts, ATT_HEADS, HEAD_DIM), v_new.reshape(db, ts, ATT_HEADS, HEAD_DIM),
                     ki_new, hlast_s[0],
                     jnp.swapaxes(ctail_s[0].reshape(CONV_W - 1, db, gw), 0, 1),
                     jnp.swapaxes(ptail_s[0].reshape(POOL_MAX - 1, db, gw), 0, 1),
                     from_rows(vn_s[0])))

    stack = lambda sts, j: jnp.stack([s[j] for s in sts], axis=0)
    return (xp, from_rows(xs[0]),
            stack(st_p, 0), stack(st_p, 1), stack(st_p, 2), stack(st_p, 3), stack(st_p, 4), stack(st_p, 5),
            stack(st_s, 0), stack(st_s, 1), stack(st_s, 2), stack(st_s, 3), stack(st_s, 4), stack(st_s, 5),
            stack(st_s, 6))
```

```python
import functools
import math

import jax
import jax.numpy as jnp
from jax import lax
from jax.experimental import pallas as pl
from jax.experimental.pallas import tpu as pltpu

F32 = jnp.float32
BF16 = jnp.bfloat16
I32 = jnp.int32

GROUP_W = 256
LRU_HEADS = 4
CONV_W = 4
RG_C = 8.0
CHUNK = 128
G_HEADS = 4
POOL_WINDOWS = (2, 4, 8, 16)
POOL_MAX = 16
ATT_HEADS = 4
HEAD_DIM = 64
IDX_HEADS = 8
IDX_DIM = 64
TOPK_MAX = 256
NUM_BUCKETS = 32
MAX_DISTANCE = 128
PAGE_SIZE = 128
LN_EPS = 1e-5

LANES = 128
SUBLANES = 8
KEY_CHUNK = 128
CHUNK_GROUP = 4
VMEM_LIMIT = 56 * 1024 * 1024

INT_MIN = -2 ** 31
NEG_BIG = -1e30
IDX_SCALE = (IDX_DIM ** -0.5) * (IDX_HEADS ** -0.5)
ATT_SCALE = HEAD_DIM ** -0.5


def _round_up(n, m):
    return (n + m - 1) // m * m


def _gelu(x):
    return x * (0.5 * (1.0 + jnp.tanh(math.sqrt(2.0 / math.pi) * (x + 0.044715 * (x * x * x)))))


def _layer_norm(x, g, b):
    mu = jnp.mean(x, axis=-1, keepdims=True)
    var = jnp.mean(jnp.square(x - mu), axis=-1, keepdims=True)
    return (x - mu) * lax.rsqrt(var + LN_EPS) * g + b


def _dot(a, b):
    return jnp.dot(a, b, preferred_element_type=F32)


def _dot_nt(a, b):
    return lax.dot_general(a, b, (((1,), (1,)), ((), ())), preferred_element_type=F32)


def _score_key(s):
    s = jnp.where(s == 0.0, 0.0, s)
    bits = lax.bitcast_convert_type(s, I32)
    return bits ^ (lax.shift_right_arithmetic(bits, 31) & 0x7FFFFFFF)


def _div_pow2(x, d):
    shift = d.bit_length() - 1
    assert 1 << shift == d
    return lax.shift_right_logical(x, shift) if shift else x


def _lane_head(shape, width):
    return _div_pow2(lax.broadcasted_iota(I32, shape, len(shape) - 1), width)


def _bias_kernel(rel_ref, tabp_ref, tabs_ref):
    def bucket(dist):
        max_exact = NUM_BUCKETS // 2
        n = jnp.maximum(dist, 0)
        nf = jnp.maximum(n, 1).astype(F32)
        large = max_exact + (jnp.log(nf / max_exact) / math.log(MAX_DISTANCE / max_exact)
                             * (NUM_BUCKETS - max_exact)).astype(I32)
        large = jnp.minimum(large, NUM_BUCKETS - 1)
        return jnp.where(n < max_exact, n, large)

    def lookup(bk, h):
        acc = jnp.zeros(bk.shape, F32)
        for j in range(NUM_BUCKETS):
            acc = jnp.where(bk == j, rel_ref[j * ATT_HEADS + h], acc)
        return acc

    r = lax.broadcasted_iota(I32, (KEY_CHUNK, KEY_CHUNK), 0)
    c = lax.broadcasted_iota(I32, (KEY_CHUNK, KEY_CHUNK), 1)
    b_prev = bucket(KEY_CHUNK + r - c)
    b_diag = bucket(r - c)
    t = lax.broadcasted_iota(I32, (SUBLANES, KEY_CHUNK), 0)
    cs = lax.broadcasted_iota(I32, (SUBLANES, KEY_CHUNK), 1)
    b_last = bucket(PAGE_SIZE + t - cs)
    b_new = bucket(t - cs)
    for h in range(ATT_HEADS):
        tabp_ref[h, 0] = lookup(b_prev, h)
        tabp_ref[h, 1] = lookup(b_diag, h)
        rows = pl.ds(h * SUBLANES, SUBLANES)
        tabs_ref[0, rows, :] = lookup(b_last, h)
        tabs_ref[1, rows, :] = lookup(b_new, h)
        tabs_ref[2, rows, :] = jnp.full((SUBLANES, KEY_CHUNK), rel_ref[(NUM_BUCKETS - 1) * ATT_HEADS + h], F32)


def _bias_tables(rel_bias):
    return pl.pallas_call(
        _bias_kernel,
        out_shape=(jax.ShapeDtypeStruct((ATT_HEADS, 2, KEY_CHUNK, KEY_CHUNK), F32),
                   jax.ShapeDtypeStruct((3, ATT_HEADS * SUBLANES, KEY_CHUNK), F32)),
        in_specs=[pl.BlockSpec(memory_space=pltpu.SMEM)],
        name="bias_tables",
    )(rel_bias.reshape(-1))


def _mixer_dims(stride):
    hc = _round_up((CONV_W - 1) * stride, SUBLANES)
    if stride % SUBLANES == 0:
        starts = tuple((w - 1) * stride for w in POOL_WINDOWS)
        hp = (POOL_MAX - 1) * stride
    else:
        starts = (SUBLANES,) * 4
        hp = _round_up(POOL_MAX - 1, SUBLANES) + SUBLANES
    return hc, hp, starts


def _mixers_kernel(*refs, stride, tb, start_pos, prompt):
    (x_ref, wabc_ref, wqkv_ref, widx_ref, wkt_ref, convw_ref, convb_ref, wgate_ref, bgate_ref, lam_ref,
     lng_ref, lnb_ref, ws_ref, bs_ref, poolw_ref, pscale_ref, conv0_ref, pool0_ref, h0_ref) = refs[:19]
    outs = refs[19:]
    if prompt:
        (yabc_ref, q_ref, k_ref, v_ref, qi_ref, kiwi_ref, ctail_ref, ptail_ref, hlast_ref,
         kt_ref, kit_ref, vbd_ref, cext, pext, l1, l2, l3, hcar) = outs
    else:
        (yabc_ref, q_ref, k_ref, v_ref, qi_ref, kiwi_ref, ctail_ref, ptail_ref, hlast_ref,
         vn_ref, cext, pext, l1, l2, l3, hcar) = outs
    hc, hp, starts = _mixer_dims(stride)
    tstep = pl.program_id(1)
    gw = GROUP_W

    @pl.when(tstep == 0)
    def _():
        cext[0:hc, :] = conv0_ref[0]
        pext[0:hp, :] = pool0_ref[0]
        hcar[...] = h0_ref[0]

    xb = x_ref[0].astype(BF16)
    za = _dot(xb, wabc_ref[...])
    zq = _dot(xb, wqkv_ref[...])
    zi = _dot(xb, widx_ref[...])
    q_ref[0] = (zq[:, 0:gw] * ATT_SCALE).astype(BF16)
    k_ref[0] = zq[:, gw:2 * gw]
    vv = zq[:, 2 * gw:3 * gw]
    v_ref[0] = vv
    qi_ref[0] = zi[:, 0:IDX_HEADS * IDX_DIM].astype(BF16)
    kiwi_ref[0] = zi[:, IDX_HEADS * IDX_DIM:]
    if prompt:
        zt = _dot_nt(wkt_ref[...], xb)
        vhead = _lane_head((KEY_CHUNK, gw), HEAD_DIM)
        for c in range(tb // KEY_CHUNK):
            cols = slice(c * KEY_CHUNK, (c + 1) * KEY_CHUNK)
            kt_ref[0, c] = zt[0:gw, cols].astype(BF16)
            kit_ref[0, c] = zt[gw:gw + IDX_DIM, cols].astype(BF16)
            for h in range(ATT_HEADS):
                vbd_ref[0, c, h * KEY_CHUNK:(h + 1) * KEY_CHUNK, :] = jnp.where(
                    vhead == h, vv[cols], 0.0).astype(BF16)

    xa = za[:, 0:gw]
    ga = za[:, gw:2 * gw]
    cext[hc:hc + tb, :] = xa
    xc = convb_ref[...] + jnp.zeros((tb, gw), F32)
    for j in range(CONV_W):
        off = hc + (j - (CONV_W - 1)) * stride
        xc = xc + cext[off:off + tb, :] * convw_ref[j:j + 1, :]
    ctail = cext[tb:tb + hc, :]
    ctail_ref[0] = ctail
    cext[0:hc, :] = ctail
    gates = _dot(xc.astype(BF16), wgate_ref[...]) + bgate_ref[...]
    r = jax.nn.sigmoid(gates[:, 0:gw])
    ig = jax.nn.sigmoid(gates[:, gw:2 * gw])
    nl = -lam_ref[...]
    softplus = jnp.maximum(nl, 0.0) + jnp.log1p(jnp.exp(-jnp.abs(nl)))
    log_a = -RG_C * r * softplus
    a = jnp.exp(log_a)
    mult = jnp.sqrt(-jnp.tanh(log_a) * (a * a + 1.0))
    bv = mult * ig * xc
    rows = lax.broadcasted_iota(I32, (tb, gw), 0)
    av = a
    d = stride
    while d < tb:
        a_sh = jnp.where(rows >= d, pltpu.roll(av, d, 0), 1.0)
        b_sh = jnp.where(rows >= d, pltpu.roll(bv, d, 0), 0.0)
        bv = av * b_sh + bv
        av = av * a_sh
        d *= 2
    hprev = hcar[...]
    if stride == 1:
        hfull = av * hprev + bv
    else:
        hfull = av * jnp.concatenate([hprev] * (tb // stride), axis=0) + bv
    hl = hfull[tb - stride:tb, :]
    hcar[...] = hl
    hlast_ref[0] = hl
    yabc_ref[0, :, 0:gw] = (hfull * _gelu(ga)).astype(BF16)

    u = _gelu(za[:, 2 * gw:3 * gw])
    vn = _layer_norm(_gelu(za[:, 3 * gw:4 * gw]), lng_ref[...], lnb_ref[...])
    if not prompt:
        vn_ref[0] = vn
    ghead = _lane_head((CHUNK, gw), gw // G_HEADS)
    for c in range(tb // CHUNK):
        crow = slice(c * CHUNK, (c + 1) * CHUNK)
        vnc = vn[crow].astype(BF16)
        mixed = bs_ref[...]
        for h in range(G_HEADS):
            mixed = mixed + jnp.where(ghead == h, _dot(ws_ref[h], vnc), 0.0)
        yabc_ref[0, crow, gw:2 * gw] = (u[crow] * mixed).astype(BF16)

    xp = za[:, 4 * gw:5 * gw]
    n0 = hp + tb
    pext[hp:n0, :] = xp
    s1, s2, s3, s4 = starts
    zero8 = jnp.zeros((max(s3, SUBLANES), gw), F32)
    l1[0:s1, :] = zero8[0:s1]
    l2[0:s2, :] = zero8[0:s2]
    l3[0:s3, :] = zero8[0:s3]
    l1[s1:n0, :] = pext[s1:n0, :] + pext[s1 - stride:n0 - stride, :]
    l2[s2:n0, :] = l1[s2:n0, :] + l1[s2 - 2 * stride:n0 - 2 * stride, :]
    l3[s3:n0, :] = l2[s3:n0, :] + l2[s3 - 4 * stride:n0 - 4 * stride, :]
    w16 = l3[hp:n0, :] + l3[hp - 8 * stride:n0 - 8 * stride, :]
    pgrp = _lane_head((tb, gw), gw // len(POOL_WINDOWS))
    sums = jnp.where(pgrp == 0, l1[hp:n0, :], jnp.where(pgrp == 1, l2[hp:n0, :],
                     jnp.where(pgrp == 2, l3[hp:n0, :], w16)))
    win = jnp.where(pgrp == 0, POOL_WINDOWS[0], jnp.where(pgrp == 1, POOL_WINDOWS[1],
                    jnp.where(pgrp == 2, POOL_WINDOWS[2], POOL_WINDOWS[3])))
    pos = start_pos + _div_pow2(tstep * tb + rows, stride)
    cnt = jnp.minimum(pos + 1, win).astype(F32)
    pooled = sums / cnt - xp
    yc = _dot(pooled.astype(BF16), poolw_ref[...]) * pscale_ref[...]
    yabc_ref[0, :, 2 * gw:3 * gw] = yc.astype(BF16)
    ptail = pext[n0 - hp:n0, :]
    ptail_ref[0] = ptail
    if tb >= hp:
        pext[0:hp, :] = ptail


def _block_diag(w):
    h, a, b = w.shape
    eye = jnp.eye(h, dtype=w.dtype)
    return (eye[:, None, :, None] * w[:, :, None, :]).reshape(h * a, h * b)


def _mixers(x, lw, conv0, pool0, h0, *, stride, tb, start_pos, prompt):
    nseq, rows, dm = x.shape
    gw = GROUP_W
    hc, hp, starts = _mixer_dims(stride)
    nt = rows // tb
    n0 = hp + tb
    kern = functools.partial(_mixers_kernel, stride=stride, tb=tb, start_pos=start_pos, prompt=prompt)

    def const(shape):
        return pl.BlockSpec(shape, lambda b, t: (0,) * len(shape))

    def per_seq(shape):
        return pl.BlockSpec((1,) + shape, lambda b, t: (b,) + (0,) * len(shape))

    def per_blk(width):
        return pl.BlockSpec((1, tb, width), lambda b, t: (b, t, 0))

    in_specs = [per_blk(dm), const(lw["w_abc"].shape), const(lw["w_qkv"].shape), const(lw["w_idx"].shape),
                const(lw["w_kt"].shape), const((SUBLANES, gw)), const((1, gw)), const((gw, 2 * gw)),
                const((1, 2 * gw)), const((1, gw)), const((1, gw)), const((1, gw)),
                const((G_HEADS, CHUNK, CHUNK)), const((CHUNK, gw)), const((gw, gw)), const((1, gw)),
                per_seq((hc, gw)), per_seq((hp, gw)), per_seq((stride, gw))]
    out_shape = [jax.ShapeDtypeStruct((nseq, rows, 3 * gw), BF16),
                 jax.ShapeDtypeStruct((nseq, rows, gw), BF16),
                 jax.ShapeDtypeStruct((nseq, rows, gw), F32),
                 jax.ShapeDtypeStruct((nseq, rows, gw), F32),
                 jax.ShapeDtypeStruct((nseq, rows, IDX_HEADS * IDX_DIM), BF16),
                 jax.ShapeDtypeStruct((nseq, rows, LANES), F32),
                 jax.ShapeDtypeStruct((nseq, hc, gw), F32),
                 jax.ShapeDtypeStruct((nseq, hp, gw), F32),
                 jax.ShapeDtypeStruct((nseq, stride, gw), F32)]
    out_specs = [per_blk(3 * gw), per_blk(gw), per_blk(gw), per_blk(gw), per_blk(IDX_HEADS * IDX_DIM),
                 per_blk(LANES), per_seq((hc, gw)), per_seq((hp, gw)), per_seq((stride, gw))]
    if prompt:
        nck = rows // KEY_CHUNK
        cpb = tb // KEY_CHUNK
        out_shape += [jax.ShapeDtypeStruct((nseq, nck, gw, KEY_CHUNK), BF16),
                      jax.ShapeDtypeStruct((nseq, nck, IDX_DIM, KEY_CHUNK), BF16),
                      jax.ShapeDtypeStruct((nseq, nck, ATT_HEADS * KEY_CHUNK, gw), BF16)]
        out_specs += [pl.BlockSpec((1, cpb, gw, KEY_CHUNK), lambda b, t: (b, t, 0, 0)),
                      pl.BlockSpec((1, cpb, IDX_DIM, KEY_CHUNK), lambda b, t: (b, t, 0, 0)),
                      pl.BlockSpec((1, cpb, ATT_HEADS * KEY_CHUNK, gw), lambda b, t: (b, t, 0, 0))]
    else:
        out_shape += [jax.ShapeDtypeStruct((nseq, rows, gw), F32)]
        out_specs += [per_blk(gw)]
    scratch = [pltpu.VMEM((hc + tb, gw), F32), pltpu.VMEM((n0, gw), F32), pltpu.VMEM((n0, gw), F32),
               pltpu.VMEM((n0, gw), F32), pltpu.VMEM((n0, gw), F32), pltpu.VMEM((stride, gw), F32)]
    return pl.pallas_call(
        kern, out_shape=out_shape, grid=(nseq, nt), in_specs=in_specs, out_specs=out_specs,
        scratch_shapes=scratch, name="mixers_prompt" if prompt else "mixers_sample",
        compiler_params=pltpu.CompilerParams(dimension_semantics=("arbitrary", "arbitrary"),
                                             vmem_limit_bytes=VMEM_LIMIT),
    )(x, lw["w_abc"], lw["w_qkv"], lw["w_idx"], lw["w_kt"], lw["conv_w"], lw["conv_b"], lw["w_gate"],
      lw["b_gate"], lw["lam"], lw["ln_g"], lw["ln_b"], lw["ws_p" if prompt else "ws_s"],
      lw["bs_p" if prompt else "bs_s"], lw["pool_w"], lw["pool_scale"], conv0, pool0, h0)


def _count_rows(acc):
    return jnp.sum(acc, axis=1, keepdims=True)


def _kth_largest_key(count_ge, shape, topk):
    def step(it, base):
        bit = lax.shift_left(jnp.int32(1), 31 - it)
        cand = jnp.where(it == 0, jnp.int32(0), base | bit)
        return jnp.where(count_ge(cand) >= topk, cand, base)
    return lax.fori_loop(0, 32, step, jnp.full(shape, INT_MIN, I32))


def _tie_aware_select(key, thr, need, carry, tri):
    eq = (key == thr) & (key != INT_MIN)
    pref = _dot(jnp.where(eq, 1.0, 0.0).astype(BF16), tri) + carry
    sel = (key > thr) | (eq & (pref <= need))
    new_carry = jnp.broadcast_to(pref[:, KEY_CHUNK - 1:KEY_CHUNK], pref.shape)
    return sel, new_carry


def _upper_tri_ones():
    r = lax.broadcasted_iota(I32, (KEY_CHUNK, KEY_CHUNK), 0)
    c = lax.broadcasted_iota(I32, (KEY_CHUNK, KEY_CHUNK), 1)
    return jnp.where(r <= c, 1.0, 0.0).astype(BF16)


def _dsa_prompt_kernel(b31_ref, qi_ref, kiwi_ref, q_ref, kit_ref, kt_ref, vbd_ref, btab_ref, out_ref,
                       qh_scr, wb_scr, qs_scr, s_scr, m_scr, l_scr, acc_scr, *, topk, qb):
    i = pl.program_id(1)
    gw = GROUP_W

    qi = qi_ref[0]
    kiwi = kiwi_ref[0]
    for h in range(IDX_HEADS):
        qh_scr[h * qb:(h + 1) * qb, :] = qi[:, h * IDX_DIM:(h + 1) * IDX_DIM]
        wcol = kiwi[:, IDX_DIM + h:IDX_DIM + h + 1] * IDX_SCALE
        wb_scr[h] = jnp.broadcast_to(wcol, (qb, LANES))
    qq = q_ref[0]
    for h in range(ATT_HEADS):
        qs_scr[h] = qq[:, h * HEAD_DIM:(h + 1) * HEAD_DIM]

    row = lax.broadcasted_iota(I32, (qb, KEY_CHUNK), 0)
    col = lax.broadcasted_iota(I32, (qb, KEY_CHUNK), 1)

    g_diag = i // CHUNK_GROUP

    def score_group(g, last):
        kit = jnp.concatenate([kit_ref[0, g * CHUNK_GROUP + c] for c in range(CHUNK_GROUP)], axis=1)
        rr = _dot(qh_scr[...], kit)
        for c in range(CHUNK_GROUP):
            j = g * CHUNK_GROUP + c
            cols = slice(c * KEY_CHUNK, (c + 1) * KEY_CHUNK)
            s = jnp.zeros((qb, KEY_CHUNK), F32)
            for h in range(IDX_HEADS):
                s = s + wb_scr[h] * jnp.maximum(rr[h * qb:(h + 1) * qb, cols], 0.0)
            key = _score_key(s)
            if last:
                diag_key = jnp.where(col <= row, key, INT_MIN)
                key = jnp.where(j < i, key, jnp.where(j == i, diag_key, INT_MIN))
            s_scr[j] = key

    def score_body(g, c):
        score_group(g, False)
        return c
    lax.fori_loop(0, g_diag, score_body, 0)
    score_group(g_diag, True)

    def count(pred):
        def body(j, acc):
            return acc + jnp.where(pred(s_scr[j]), 1.0, 0.0)
        return _count_rows(lax.fori_loop(0, i + 1, body, jnp.zeros((qb, KEY_CHUNK), F32)))

    thr = _kth_largest_key(lambda cand: count(lambda k: k >= cand), (qb, KEY_CHUNK), topk)
    need = jnp.broadcast_to(topk - count(lambda k: k > thr), (qb, KEY_CHUNK))

    m_scr[...] = jnp.full(m_scr.shape, NEG_BIG, F32)
    l_scr[...] = jnp.zeros(l_scr.shape, F32)
    acc_scr[...] = jnp.zeros(acc_scr.shape, F32)
    tri = _upper_tri_ones()
    g_far = jnp.maximum(i - 1, 0) // CHUNK_GROUP

    def logits(j, h):
        return _dot(qs_scr[h], kt_ref[0, j, pl.ds(h * HEAD_DIM, HEAD_DIM), :])

    def near_bias(j, h):
        tab = btab_ref[h, jnp.clip(j - (i - 1), 0, 1)]
        return jnp.where(j >= i - 1, tab - b31_ref[h], 0.0)

    def max_group(g, carry, near):
        js = [g * CHUNK_GROUP + c for c in range(CHUNK_GROUP)]
        keys = [s_scr[j] for j in js]
        eqs = [(k == thr) & (k != INT_MIN) for k in keys]
        prefs = [_dot(jnp.where(e, 1.0, 0.0).astype(BF16), tri) for e in eqs]
        vmax = [None] * ATT_HEADS
        for c, j in enumerate(js):
            pref = prefs[c] + carry
            sel = (keys[c] > thr) | (eqs[c] & (pref <= need))
            carry = jnp.broadcast_to(pref[:, KEY_CHUNK - 1:KEY_CHUNK], pref.shape)
            madd = jnp.where(sel, 0.0, NEG_BIG)
            s_scr[j] = lax.bitcast_convert_type(madd, I32)
            for h in range(ATT_HEADS):
                v = logits(j, h) + madd
                if near:
                    v = v + near_bias(j, h)
                vmax[h] = v if vmax[h] is None else jnp.maximum(vmax[h], v)
        for h in range(ATT_HEADS):
            m_scr[h] = jnp.maximum(m_scr[h], vmax[h])
        return carry

    carry = lax.fori_loop(0, g_far, lambda g, c: max_group(g, c, False), jnp.zeros((qb, KEY_CHUNK), F32))
    lax.fori_loop(g_far, g_diag + 1, lambda g, c: max_group(g, c, True), carry)
    for h in range(ATT_HEADS):
        m_row = jnp.max(m_scr[h], axis=1, keepdims=True)
        m_scr[h] = jnp.broadcast_to(-m_row, (qb, LANES))

    def exp_group(g, near):
        pv = None
        lsum = [None] * ATT_HEADS
        for c in range(CHUNK_GROUP):
            j = g * CHUNK_GROUP + c
            madd = lax.bitcast_convert_type(s_scr[j], F32)
            ps = []
            for h in range(ATT_HEADS):
                shift = m_scr[h] + near_bias(j, h) if near else m_scr[h]
                p = jnp.exp(logits(j, h) + (madd + shift))
                lsum[h] = p if lsum[h] is None else lsum[h] + p
                ps.append(p.astype(BF16))
            d = _dot(jnp.concatenate(ps, axis=1), vbd_ref[0, j])
            pv = d if pv is None else pv + d
        for h in range(ATT_HEADS):
            l_scr[h] = l_scr[h] + lsum[h]
        acc_scr[...] = acc_scr[...] + pv

    def far_body(g, c):
        exp_group(g, False)
        return c

    def near_body(g, c):
        exp_group(g, True)
        return c
    lax.fori_loop(0, g_far, far_body, 0)
    lax.fori_loop(g_far, g_diag + 1, near_body, 0)
    lane64 = lax.broadcasted_iota(I32, (qb, LANES), 1) < HEAD_DIM
    ls = [jnp.broadcast_to(jnp.sum(l_scr[h], axis=1, keepdims=True), (qb, LANES)) for h in range(ATT_HEADS)]
    l_cat = jnp.concatenate([jnp.where(lane64, ls[0], ls[1]), jnp.where(lane64, ls[2], ls[3])], axis=1)
    out_ref[0] = (acc_scr[...] / l_cat).astype(BF16)


def _dsa_prompt(qi, kiwi, q, kit, kt, vbd, btab, b31, *, topk):
    nb, t, _ = q.shape
    qb = KEY_CHUNK
    nq = t // qb
    gw = GROUP_W
    kern = functools.partial(_dsa_prompt_kernel, topk=topk, qb=qb)

    def per_q(width):
        return pl.BlockSpec((1, qb, width), lambda b, i: (b, i, 0))

    def per_batch(shape):
        return pl.BlockSpec((1,) + shape, lambda b, i: (b,) + (0,) * len(shape))

    in_specs = [pl.BlockSpec(memory_space=pltpu.SMEM),
                per_q(IDX_HEADS * IDX_DIM), per_q(LANES), per_q(gw),
                per_batch((nq, IDX_DIM, KEY_CHUNK)), per_batch((nq, gw, KEY_CHUNK)),
                pl.BlockSpec((1, nq, ATT_HEADS * KEY_CHUNK, gw), lambda b, i: (b, 0, 0, 0),
                             pipeline_mode=pl.Buffered(1)),
                pl.BlockSpec((ATT_HEADS, 2, KEY_CHUNK, KEY_CHUNK), lambda b, i: (0, 0, 0, 0))]
    scratch = [pltpu.VMEM((IDX_HEADS * qb, IDX_DIM), BF16), pltpu.VMEM((IDX_HEADS, qb, LANES), F32),
               pltpu.VMEM((ATT_HEADS, qb, HEAD_DIM), BF16), pltpu.VMEM((nq, qb, KEY_CHUNK), I32),
               pltpu.VMEM((ATT_HEADS, qb, LANES), F32), pltpu.VMEM((ATT_HEADS, qb, LANES), F32),
               pltpu.VMEM((qb, gw), F32)]
    return pl.pallas_call(
        kern, out_shape=jax.ShapeDtypeStruct((nb, t, gw), BF16), grid=(nb, nq),
        in_specs=in_specs, out_specs=per_q(gw), scratch_shapes=scratch, name="dsa_prompt",
        compiler_params=pltpu.CompilerParams(dimension_semantics=("arbitrary", "arbitrary"),
                                             vmem_limit_bytes=VMEM_LIMIT),
    )(b31, qi, kiwi, q, kit, kt, vbd, btab)


PAGES_PER_STEP = 8


def _sample_scores(qi, wpad, keys_t):
    rr = _dot(qi, keys_t)
    s = jnp.zeros((SUBLANES, rr.shape[1]), F32)
    for h in range(IDX_HEADS):
        rows = slice(h * SUBLANES, (h + 1) * SUBLANES)
        s = s + wpad[rows, 0:1] * jnp.maximum(rr[rows], 0.0)
    return _score_key(s)


def _dsa_sample_index_kernel(pt_ref, qi_ref, wpad_ref, *refs, npages, topk):
    g = PAGES_PER_STEP
    page_refs = refs[:g]
    kinew_ref, spast_ref, snew_ref, thr_ref, need_ref = refs[g:]
    s = pl.program_id(1)
    qi = qi_ref[0]
    wpad = wpad_ref[0]
    pages = jnp.concatenate([page_refs[p][0, 0].astype(BF16) for p in range(g)], axis=1)
    keys = _sample_scores(qi, wpad, pages)
    for p in range(g):
        spast_ref[0, s * g + p] = keys[:, p * KEY_CHUNK:(p + 1) * KEY_CHUNK]

    @pl.when(s == pl.num_programs(1) - 1)
    def _():
        row = lax.broadcasted_iota(I32, (SUBLANES, KEY_CHUNK), 0)
        col = lax.broadcasted_iota(I32, (SUBLANES, KEY_CHUNK), 1)
        knew = jnp.where(col <= row, _sample_scores(qi, wpad, kinew_ref[0]), INT_MIN)
        snew_ref[0] = knew

        def count(pred):
            def body(j, acc):
                return acc + jnp.where(pred(spast_ref[0, j]), 1.0, 0.0)
            acc = lax.fori_loop(0, npages, body, jnp.where(pred(knew), 1.0, 0.0))
            return _count_rows(acc)

        thr = _kth_largest_key(lambda cand: count(lambda k: k >= cand), (SUBLANES, KEY_CHUNK), topk)
        thr_ref[0] = thr
        need_ref[0] = jnp.broadcast_to(topk - count(lambda k: k > thr), (SUBLANES, KEY_CHUNK))


def _dsa_sample_attend_kernel(pt_ref, qbd_ref, spast_ref, snew_ref, thr_ref, need_ref, *refs, npages):
    g = PAGES_PER_STEP
    k_refs = refs[:g]
    v_refs = refs[g:2 * g]
    knew_ref, vnew_ref, tabs_ref, out_ref, m_scr, l_scr, acc_scr, carry_scr = refs[2 * g:]
    s = pl.program_id(1)
    nh = ATT_HEADS * SUBLANES

    @pl.when(s == 0)
    def _():
        m_scr[...] = jnp.full(m_scr.shape, NEG_BIG, F32)
        l_scr[...] = jnp.zeros(l_scr.shape, F32)
        acc_scr[...] = jnp.zeros(acc_scr.shape, F32)
        carry_scr[...] = jnp.zeros(carry_scr.shape, F32)

    qbd = qbd_ref[0]
    thr = thr_ref[0]
    need = need_ref[0]
    tri = _upper_tri_ones()

    def select(key):
        sel8, carry = _tie_aware_select(key, thr, need, carry_scr[...], tri)
        carry_scr[...] = carry
        return jnp.concatenate([jnp.where(sel8, 1, 0)] * ATT_HEADS, axis=0)

    def attend(sel, kt, vt, bias):
        lg = jnp.where(sel > 0, _dot(qbd, kt) + bias, NEG_BIG)
        m_old = m_scr[...]
        m_new = jnp.maximum(m_old, jnp.max(lg, axis=1, keepdims=True))
        alpha = jnp.exp(m_old - m_new)
        p = jnp.where(sel > 0, jnp.exp(lg - m_new[:, 0:1]), 0.0)
        l_scr[...] = alpha * l_scr[...] + jnp.sum(p, axis=1, keepdims=True)
        m_scr[...] = m_new
        alpha2 = jnp.concatenate([alpha, alpha], axis=1)
        acc_scr[...] = acc_scr[...] * alpha2 + _dot_nt(p.astype(BF16), vt)

    sels, biases = [], []
    for p in range(g):
        sels.append(select(spast_ref[0, p]))
        biases.append(jnp.where(s * g + p == npages - 1, tabs_ref[0], tabs_ref[2]))
    attend(jnp.concatenate(sels, axis=1),
           jnp.concatenate([k_refs[p][0, 0].astype(BF16) for p in range(g)], axis=1),
           jnp.concatenate([v_refs[p][0, 0].astype(BF16) for p in range(g)], axis=1),
           jnp.concatenate(biases, axis=1))

    @pl.when(s == pl.num_programs(1) - 1)
    def _():
        attend(select(snew_ref[0]), knew_ref[0], vnew_ref[0], tabs_ref[1])
        l2 = jnp.concatenate([l_scr[...], l_scr[...]], axis=1)
        o = acc_scr[...] / l2
        head = _lane_head((SUBLANES, GROUP_W), HEAD_DIM)
        out = jnp.zeros((SUBLANES, GROUP_W), F32)
        for h in range(ATT_HEADS):
            out = out + jnp.where(head == h, o[h * SUBLANES:(h + 1) * SUBLANES], 0.0)
        out_ref[0] = out


def _dsa_sample(layer, page_table, qi_pad, wpad, kinew, qbd, knew, vnew, cache_ik, cache_k, cache_v, tabs, *, topk):
    db, npages = page_table.shape
    g = PAGES_PER_STEP
    ns = npages // g
    gw = GROUP_W
    pt = page_table.reshape(-1)

    def page_spec(width, p):
        return pl.BlockSpec((1, 1, width, PAGE_SIZE),
                            lambda b, s, pt_ref: (layer, pt_ref[b * npages + s * g + p], 0, 0))

    def per_b(shape):
        return pl.BlockSpec((1,) + shape, lambda b, s, pt_ref: (b,) + (0,) * len(shape))

    index_kern = functools.partial(_dsa_sample_index_kernel, npages=npages, topk=topk)
    spast, snew, thr, need = pl.pallas_call(
        index_kern,
        out_shape=(jax.ShapeDtypeStruct((db, npages, SUBLANES, KEY_CHUNK), I32),
                   jax.ShapeDtypeStruct((db, SUBLANES, KEY_CHUNK), I32),
                   jax.ShapeDtypeStruct((db, SUBLANES, KEY_CHUNK), I32),
                   jax.ShapeDtypeStruct((db, SUBLANES, KEY_CHUNK), F32)),
        grid_spec=pltpu.PrefetchScalarGridSpec(
            num_scalar_prefetch=1, grid=(db, ns),
            in_specs=[per_b((IDX_HEADS * SUBLANES, IDX_DIM)), per_b((IDX_HEADS * SUBLANES, LANES))]
                     + [page_spec(IDX_DIM, p) for p in range(g)] + [per_b((IDX_DIM, KEY_CHUNK))],
            out_specs=[per_b((npages, SUBLANES, KEY_CHUNK)), per_b((SUBLANES, KEY_CHUNK)),
                       per_b((SUBLANES, KEY_CHUNK)), per_b((SUBLANES, KEY_CHUNK))]),
        name="dsa_sample_index",
        compiler_params=pltpu.CompilerParams(dimension_semantics=("arbitrary", "arbitrary"),
                                             vmem_limit_bytes=VMEM_LIMIT),
    )(pt, qi_pad, wpad, *([cache_ik] * g), kinew)

    attend_kern = functools.partial(_dsa_sample_attend_kernel, npages=npages)
    nh = ATT_HEADS * SUBLANES
    return pl.pallas_call(
        attend_kern,
        out_shape=jax.ShapeDtypeStruct((db, SUBLANES, gw), F32),
        grid_spec=pltpu.PrefetchScalarGridSpec(
            num_scalar_prefetch=1, grid=(db, ns),
            in_specs=[per_b((nh, gw)),
                      pl.BlockSpec((1, g, SUBLANES, KEY_CHUNK), lambda b, s, pt_ref: (b, s, 0, 0)),
                      per_b((SUBLANES, KEY_CHUNK)), per_b((SUBLANES, KEY_CHUNK)), per_b((SUBLANES, KEY_CHUNK))]
                     + [page_spec(gw, p) for p in range(g)] + [page_spec(gw, p) for p in range(g)]
                     + [per_b((gw, KEY_CHUNK)), per_b((gw, KEY_CHUNK)),
                        pl.BlockSpec((3, nh, KEY_CHUNK), lambda b, s, pt_ref: (0, 0, 0))],
            out_specs=per_b((SUBLANES, gw)),
            scratch_shapes=[pltpu.VMEM((nh, LANES), F32), pltpu.VMEM((nh, LANES), F32),
                            pltpu.VMEM((nh, gw), F32), pltpu.VMEM((SUBLANES, KEY_CHUNK), F32)]),
        name="dsa_sample_attend",
        compiler_params=pltpu.CompilerParams(dimension_semantics=("arbitrary", "arbitrary"),
                                             vmem_limit_bytes=VMEM_LIMIT),
    )(pt, qbd, spast, snew, thr, need, *([cache_k] * g), *([cache_v] * g), knew, vnew, tabs)


FF_CHUNK = 1024


def _ffn_kernel(x_ref, yabc_ref, yd_ref, woa_ref, wod_ref, g1_ref, b1_ref, w1_ref, w2_ref, g2_ref, b2_ref,
                out_ref, *, alpha):
    x = x_ref[...]
    mix = _dot(yabc_ref[...], woa_ref[...]) + _dot(yd_ref[...], wod_ref[...])
    x1 = _layer_norm(alpha * x + mix, g1_ref[...], b1_ref[...])
    x1b = x1.astype(BF16)
    acc = jnp.zeros(x.shape, F32)
    for c in range(w1_ref.shape[1] // FF_CHUNK):
        cols = slice(c * FF_CHUNK, (c + 1) * FF_CHUNK)
        hdn = jnp.square(jnp.maximum(_dot(x1b, w1_ref[:, cols]), 0.0))
        acc = acc + _dot(hdn.astype(BF16), w2_ref[cols, :])
    out_ref[...] = _layer_norm(alpha * x1 + acc, g2_ref[...], b2_ref[...])


def _ffn(x, yabc, yd, lw, *, tm, alpha):
    rows, dm = x.shape
    dff = lw["w1"].shape[1]

    def blk(width):
        return pl.BlockSpec((tm, width), lambda r: (r, 0))

    def const(shape):
        return pl.BlockSpec(shape, lambda r: (0,) * len(shape), pipeline_mode=pl.Buffered(1))

    return pl.pallas_call(
        functools.partial(_ffn_kernel, alpha=alpha),
        out_shape=jax.ShapeDtypeStruct((rows, dm), F32), grid=(rows // tm,),
        in_specs=[blk(dm), blk(3 * GROUP_W), blk(GROUP_W), const((3 * GROUP_W, dm)), const((GROUP_W, dm)),
                  const((1, dm)), const((1, dm)), const((dm, dff)), const((dff, dm)), const((1, dm)),
                  const((1, dm))],
        out_specs=blk(dm), name="ffn",
        compiler_params=pltpu.CompilerParams(dimension_semantics=("arbitrary",),
                                             vmem_limit_bytes=VMEM_LIMIT),
    )(x, yabc, yd, lw["w_out_abc"], lw["w_out_d"], lw["ln1_g"], lw["ln1_b"], lw["w1"], lw["w2"],
      lw["ln2_g"], lw["ln2_b"])


def _layer_weights(l, dec_batch, dec_seq, w_in, conv_w, conv_b, lru_wa, lru_ba, lru_wx, lru_bx, lru_lambda,
                   gmlp_ln_g, gmlp_ln_b, gmlp_ws, gmlp_bs, pool_w, pool_scale, w_out, ln1_g, ln1_b, ln2_g,
                   ln2_b, ffn_w1, ffn_w2):
    gw = GROUP_W
    dm = w_in.shape[1]
    wi = w_in[l]
    o_q, o_k, o_v = 5 * gw, 6 * gw, 7 * gw
    o_qi = 8 * gw
    o_ki = o_qi + IDX_HEADS * IDX_DIM
    o_wi = o_ki + IDX_DIM
    pad = jnp.zeros((dm, LANES - IDX_DIM - IDX_HEADS), wi.dtype)
    row = lambda a: a[l].reshape(1, -1)
    tril = jnp.tril(jnp.ones((CHUNK, CHUNK), F32))
    ws_tril = gmlp_ws[l] * tril
    eye_b = jnp.eye(dec_batch, dtype=F32)
    ws_small = ws_tril[:, :dec_seq, :dec_seq]
    ws_s = (ws_small[:, :, None, :, None] * eye_b[None, None, :, None, :]).reshape(
        G_HEADS, dec_seq * dec_batch, dec_seq * dec_batch)
    bs_cols = jnp.repeat(gmlp_bs[l].T, gw // G_HEADS, axis=1)
    return {
        "w_abc": wi[:, :o_q].astype(BF16),
        "w_qkv": wi[:, o_q:o_qi].astype(BF16),
        "w_idx": jnp.concatenate([wi[:, o_qi:], pad], axis=1).astype(BF16),
        "w_kt": jnp.concatenate([wi[:, o_k:o_v], wi[:, o_ki:o_wi]], axis=1).T.astype(BF16),
        "conv_w": jnp.concatenate([conv_w[l], jnp.zeros((SUBLANES - CONV_W, gw), F32)], axis=0),
        "conv_b": row(conv_b),
        "w_gate": jnp.concatenate([_block_diag(lru_wa[l]), _block_diag(lru_wx[l])], axis=1).astype(BF16),
        "b_gate": jnp.concatenate([lru_ba[l], lru_bx[l]]).reshape(1, -1),
        "lam": row(lru_lambda), "ln_g": row(gmlp_ln_g), "ln_b": row(gmlp_ln_b),
        "ws_p": ws_tril.astype(BF16), "ws_s": ws_s.astype(BF16),
        "bs_p": bs_cols, "bs_s": jnp.repeat(bs_cols[:dec_seq], dec_batch, axis=0),
        "pool_w": _block_diag(pool_w[l]).astype(BF16), "pool_scale": row(pool_scale),
        "w_out_abc": w_out[l, :3 * gw].astype(BF16), "w_out_d": w_out[l, 3 * gw:].astype(BF16),
        "ln1_g": row(ln1_g), "ln1_b": row(ln1_b), "ln2_g": row(ln2_g), "ln2_b": row(ln2_b),
        "w1": ffn_w1[l].astype(BF16), "w2": ffn_w2[l].astype(BF16),
    }


PROMPT_TB = 256
FFN_TM = 512


def kernel(x_prompt, x_sample, cache_k, cache_v, cache_idx_k, state_lru_h, state_conv, state_pool, page_table,
           w_in, conv_w, conv_b, lru_wa, lru_ba, lru_wx, lru_bx, lru_lambda, gmlp_ln_g, gmlp_ln_b, gmlp_ws,
           gmlp_bs, pool_w, pool_scale, rel_bias, w_out, ln1_g, ln1_b, ln2_g, ln2_b, ffn_w1, ffn_w2):
    nb, t, dm = x_prompt.shape
    db, ts, _ = x_sample.shape
    depth = w_in.shape[0]
    gw = GROUP_W
    npages = page_table.shape[1]
    past_len = npages * PAGE_SIZE
    topk_p = min(TOPK_MAX, t // 4)
    topk_s = min(TOPK_MAX, (past_len + ts) // 4)
    alpha = (2 * depth) ** 0.25
    n_pool = cache_k.shape[1]
    cache_kt = jnp.transpose(cache_k, (0, 1, 3, 4, 2)).reshape(depth, n_pool, gw, PAGE_SIZE)
    cache_vt = jnp.transpose(cache_v, (0, 1, 3, 4, 2)).reshape(depth, n_pool, gw, PAGE_SIZE)
    cache_ikt = jnp.swapaxes(cache_idx_k, 2, 3)

    btab, tabs = _bias_tables(rel_bias)
    b31 = rel_bias[NUM_BUCKETS - 1]
    hc_p, hp_p, _ = _mixer_dims(1)
    hc_s, hp_s, _ = _mixer_dims(db)
    srows = ts * db

    def to_rows(a):
        return jnp.swapaxes(a, 0, 1).reshape((srows,) + a.shape[2:])

    def from_rows(a):
        return jnp.swapaxes(a.reshape((ts, db) + a.shape[1:]), 0, 1)

    xp = x_prompt
    xs = to_rows(x_sample)[None]
    st_p, st_s = [], []
    for l in range(depth):
        lw = _layer_weights(l, db, ts, w_in, conv_w, conv_b, lru_wa, lru_ba, lru_wx, lru_bx, lru_lambda,
                            gmlp_ln_g, gmlp_ln_b, gmlp_ws, gmlp_bs, pool_w, pool_scale, w_out, ln1_g, ln1_b,
                            ln2_g, ln2_b, ffn_w1, ffn_w2)
        (yabc, q, k, v, qi, kiwi, ctail, ptail, hlast, kt, kit, vbd) = _mixers(
            xp, lw, jnp.zeros((nb, hc_p, gw), F32), jnp.zeros((nb, hp_p, gw), F32), jnp.zeros((nb, 1, gw), F32),
            stride=1, tb=PROMPT_TB, start_pos=0, prompt=True)
        yd = _dsa_prompt(qi, kiwi, q, kit, kt, vbd, btab, b31, topk=topk_p)
        xp = _ffn(xp.reshape(nb * t, dm), yabc.reshape(nb * t, 3 * gw), yd.reshape(nb * t, gw), lw,
                  tm=FFN_TM, alpha=alpha).reshape(nb, t, dm)
        st_p.append((k.reshape(nb, t, ATT_HEADS, HEAD_DIM), v.reshape(nb, t, ATT_HEADS, HEAD_DIM),
                     kiwi[:, :, :IDX_DIM], hlast[:, 0], ctail[:, hc_p - (CONV_W - 1):],
                     ptail[:, hp_p - (POOL_MAX - 1):]))
        conv0 = jnp.swapaxes(state_conv[l], 0, 1).reshape(1, hc_s, gw)
        pool0 = jnp.swapaxes(state_pool[l], 0, 1).reshape(1, hp_s, gw)
        (yabc_s, q_s, k_s, v_s, qi_s, kiwi_s, ctail_s, ptail_s, hlast_s, vn_s) = _mixers(
            xs, lw, conv0, pool0, state_lru_h[l][None], stride=db, tb=srows, start_pos=past_len, prompt=False)
        k_new = from_rows(k_s[0])
        v_new = from_rows(v_s[0])
        ki_new = from_rows(kiwi_s[0, :, :IDX_DIM])
        w_new = from_rows(kiwi_s[0, :, IDX_DIM:IDX_DIM + IDX_HEADS])
        qi_new = from_rows(qi_s[0]).reshape(db, ts, IDX_HEADS, IDX_DIM)
        q_new = from_rows(q_s[0]).reshape(db, ts, ATT_HEADS, HEAD_DIM)
        tpad = SUBLANES - ts
        qi_pad = jnp.pad(jnp.swapaxes(qi_new, 1, 2), ((0, 0), (0, 0), (0, tpad), (0, 0))).reshape(
            db, IDX_HEADS * SUBLANES, IDX_DIM)
        wpad = jnp.pad(jnp.swapaxes(w_new, 1, 2) * IDX_SCALE, ((0, 0), (0, 0), (0, tpad))).reshape(
            db, IDX_HEADS * SUBLANES, 1)
        wpad = jnp.broadcast_to(wpad, (db, IDX_HEADS * SUBLANES, LANES))
        q_heads = jnp.pad(jnp.swapaxes(q_new, 1, 2), ((0, 0), (0, 0), (0, tpad), (0, 0)))
        eye_h = jnp.eye(ATT_HEADS, dtype=q_heads.dtype)
        qbd = (q_heads[:, :, :, None, :] * eye_h[None, :, None, :, None]).reshape(
            db, ATT_HEADS * SUBLANES, gw)
        new_t = lambda a: jnp.swapaxes(jnp.pad(a, ((0, 0), (0, KEY_CHUNK - ts), (0, 0))), 1, 2).astype(BF16)
        yd_s = _dsa_sample(l, page_table, qi_pad, wpad, new_t(ki_new), qbd, new_t(k_new), new_t(v_new),
                           cache_ikt, cache_kt, cache_vt, tabs, topk=topk_s)
        yd_rows = to_rows(yd_s[:, :ts]).astype(BF16)
        xs = _ffn(xs[0], yabc_s[0], yd_rows, lw, tm=srows, alpha=alpha)[None]
        st_s.append((k_new.reshape(db, ts, ATT_HEADS, HEAD_DIM), v_new.reshape(db, ts, ATT_HEADS, HEAD_DIM),
                     ki_new, hlast_s[0],
                     jnp.swapaxes(ctail_s[0].reshape(CONV_W - 1, db, gw), 0, 1),
                     jnp.swapaxes(ptail_s[0].reshape(POOL_MAX - 1, db, gw), 0, 1),
                     from_rows(vn_s[0])))

    stack = lambda sts, j: jnp.stack([s[j] for s in sts], axis=0)
    return (xp, from_rows(xs[0]),
            stack(st_p, 0), stack(st_p, 1), stack(st_p, 2), stack(st_p, 3), stack(st_p, 4), stack(st_p, 5),
            stack(st_s, 0), stack(st_s, 1), stack(st_s, 2), stack(st_s, 3), stack(st_s, 4), stack(st_s, 5),
            stack(st_s, 6))
```

```python
import functools
import math

import jax
import jax.numpy as jnp
from jax import lax
from jax.experimental import pallas as pl
from jax.experimental.pallas import tpu as pltpu

F32 = jnp.float32
BF16 = jnp.bfloat16
I32 = jnp.int32

GROUP_W = 256
LRU_HEADS = 4
CONV_W = 4
RG_C = 8.0
CHUNK = 128
G_HEADS = 4
POOL_WINDOWS = (2, 4, 8, 16)
POOL_MAX = 16
ATT_HEADS = 4
HEAD_DIM = 64
IDX_HEADS = 8
IDX_DIM = 64
TOPK_MAX = 256
NUM_BUCKETS = 32
MAX_DISTANCE = 128
PAGE_SIZE = 128
LN_EPS = 1e-5

LANES = 128
SUBLANES = 8
KEY_CHUNK = 128
CHUNK_GROUP = 4
VMEM_LIMIT = 56 * 1024 * 1024

INT_MIN = -2 ** 31
NEG_BIG = -1e30
IDX_SCALE = (IDX_DIM ** -0.5) * (IDX_HEADS ** -0.5)
ATT_SCALE = HEAD_DIM ** -0.5


def _round_up(n, m):
    return (n + m - 1) // m * m


def _gelu(x):
    return x * (0.5 * (1.0 + jnp.tanh(math.sqrt(2.0 / math.pi) * (x + 0.044715 * (x * x * x)))))


def _layer_norm(x, g, b):
    mu = jnp.mean(x, axis=-1, keepdims=True)
    var = jnp.mean(jnp.square(x - mu), axis=-1, keepdims=True)
    return (x - mu) * lax.rsqrt(var + LN_EPS) * g + b


def _dot(a, b):
    return jnp.dot(a, b, preferred_element_type=F32)


def _dot_nt(a, b):
    return lax.dot_general(a, b, (((1,), (1,)), ((), ())), preferred_element_type=F32)


def _score_key(s):
    s = jnp.where(s == 0.0, 0.0, s)
    bits = lax.bitcast_convert_type(s, I32)
    return bits ^ (lax.shift_right_arithmetic(bits, 31) & 0x7FFFFFFF)


def _div_pow2(x, d):
    shift = d.bit_length() - 1
    assert 1 << shift == d
    return lax.shift_right_logical(x, shift) if shift else x


def _lane_head(shape, width):
    return _div_pow2(lax.broadcasted_iota(I32, shape, len(shape) - 1), width)


def _bias_kernel(rel_ref, tabp_ref, tabs_ref):
    def bucket(dist):
        max_exact = NUM_BUCKETS // 2
        n = jnp.maximum(dist, 0)
        nf = jnp.maximum(n, 1).astype(F32)
        large = max_exact + (jnp.log(nf / max_exact) / math.log(MAX_DISTANCE / max_exact)
                             * (NUM_BUCKETS - max_exact)).astype(I32)
        large = jnp.minimum(large, NUM_BUCKETS - 1)
        return jnp.where(n < max_exact, n, large)

    def lookup(bk, h):
        acc = jnp.zeros(bk.shape, F32)
        for j in range(NUM_BUCKETS):
            acc = jnp.where(bk == j, rel_ref[j * ATT_HEADS + h], acc)
        return acc

    r = lax.broadcasted_iota(I32, (KEY_CHUNK, KEY_CHUNK), 0)
    c = lax.broadcasted_iota(I32, (KEY_CHUNK, KEY_CHUNK), 1)
    b_prev = bucket(KEY_CHUNK + r - c)
    b_diag = bucket(r - c)
    t = lax.broadcasted_iota(I32, (SUBLANES, KEY_CHUNK), 0)
    cs = lax.broadcasted_iota(I32, (SUBLANES, KEY_CHUNK), 1)
    b_last = bucket(PAGE_SIZE + t - cs)
    b_new = bucket(t - cs)
    for h in range(ATT_HEADS):
        tabp_ref[h, 0] = lookup(b_prev, h)
        tabp_ref[h, 1] = lookup(b_diag, h)
        rows = pl.ds(h * SUBLANES, SUBLANES)
        tabs_ref[0, rows, :] = lookup(b_last, h)
        tabs_ref[1, rows, :] = lookup(b_new, h)
        tabs_ref[2, rows, :] = jnp.full((SUBLANES, KEY_CHUNK), rel_ref[(NUM_BUCKETS - 1) * ATT_HEADS + h], F32)


def _bias_tables(rel_bias):
    return pl.pallas_call(
        _bias_kernel,
        out_shape=(jax.ShapeDtypeStruct((ATT_HEADS, 2, KEY_CHUNK, KEY_CHUNK), F32),
                   jax.ShapeDtypeStruct((3, ATT_HEADS * SUBLANES, KEY_CHUNK), F32)),
        in_specs=[pl.BlockSpec(memory_space=pltpu.SMEM)],
        name="bias_tables",
    )(rel_bias.reshape(-1))


def _mixer_dims(stride):
    hc = _round_up((CONV_W - 1) * stride, SUBLANES)
    if stride % SUBLANES == 0:
        starts = tuple((w - 1) * stride for w in POOL_WINDOWS)
        hp = (POOL_MAX - 1) * stride
    else:
        starts = (SUBLANES,) * 4
        hp = _round_up(POOL_MAX - 1, SUBLANES) + SUBLANES
    return hc, hp, starts


def _mixers_kernel(*refs, stride, tb, start_pos, prompt):
    (x_ref, wabc_ref, wqkv_ref, widx_ref, wkt_ref, convw_ref, convb_ref, wgate_ref, bgate_ref, lam_ref,
     lng_ref, lnb_ref, ws_ref, bs_ref, poolw_ref, pscale_ref, conv0_ref, pool0_ref, h0_ref) = refs[:19]
    outs = refs[19:]
    if prompt:
        (yabc_ref, q_ref, k_ref, v_ref, qi_ref, kiwi_ref, ctail_ref, ptail_ref, hlast_ref,
         kt_ref, kit_ref, vbd_ref, cext, pext, l1, l2, l3, hcar) = outs
    else:
        (yabc_ref, q_ref, k_ref, v_ref, qi_ref, kiwi_ref, ctail_ref, ptail_ref, hlast_ref,
         vn_ref, cext, pext, l1, l2, l3, hcar) = outs
    hc, hp, starts = _mixer_dims(stride)
    tstep = pl.program_id(1)
    gw = GROUP_W

    @pl.when(tstep == 0)
    def _():
        cext[0:hc, :] = conv0_ref[0]
        pext[0:hp, :] = pool0_ref[0]
        hcar[...] = h0_ref[0]

    xb = x_ref[0].astype(BF16)
    za = _dot(xb, wabc_ref[...])
    zq = _dot(xb, wqkv_ref[...])
    zi = _dot(xb, widx_ref[...])
    q_ref[0] = (zq[:, 0:gw] * ATT_SCALE).astype(BF16)
    k_ref[0] = zq[:, gw:2 * gw]
    vv = zq[:, 2 * gw:3 * gw]
    v_ref[0] = vv
    qi_ref[0] = zi[:, 0:IDX_HEADS * IDX_DIM].astype(BF16)
    kiwi_ref[0] = zi[:, IDX_HEADS * IDX_DIM:]
    if prompt:
        zt = _dot_nt(wkt_ref[...], xb)
        vhead = _lane_head((KEY_CHUNK, gw), HEAD_DIM)
        for c in range(tb // KEY_CHUNK):
            cols = slice(c * KEY_CHUNK, (c + 1) * KEY_CHUNK)
            kt_ref[0, c] = zt[0:gw, cols].astype(BF16)
            kit_ref[0, c] = zt[gw:gw + IDX_DIM, cols].astype(BF16)
            for h in range(ATT_HEADS):
                vbd_ref[0, c, h * KEY_CHUNK:(h + 1) * KEY_CHUNK, :] = jnp.where(
                    vhead == h, vv[cols], 0.0).astype(BF16)

    xa = za[:, 0:gw]
    ga = za[:, gw:2 * gw]
    cext[hc:hc + tb, :] = xa
    xc = convb_ref[...] + jnp.zeros((tb, gw), F32)
    for j in range(CONV_W):
        off = hc + (j - (CONV_W - 1)) * stride
        xc = xc + cext[off:off + tb, :] * convw_ref[j:j + 1, :]
    ctail = cext[tb:tb + hc, :]
    ctail_ref[0] = ctail
    cext[0:hc, :] = ctail
    gates = _dot(xc.astype(BF16), wgate_ref[...]) + bgate_ref[...]
    r = jax.nn.sigmoid(gates[:, 0:gw])
    ig = jax.nn.sigmoid(gates[:, gw:2 * gw])
    nl = -lam_ref[...]
    softplus = jnp.maximum(nl, 0.0) + jnp.log1p(jnp.exp(-jnp.abs(nl)))
    log_a = -RG_C * r * softplus
    a = jnp.exp(log_a)
    mult = jnp.sqrt(-jnp.tanh(log_a) * (a * a + 1.0))
    bv = mult * ig * xc
    rows = lax.broadcasted_iota(I32, (tb, gw), 0)
    av = a
    d = stride
    while d < tb:
        a_sh = jnp.where(rows >= d, pltpu.roll(av, d, 0), 1.0)
        b_sh = jnp.where(rows >= d, pltpu.roll(bv, d, 0), 0.0)
        bv = av * b_sh + bv
        av = av * a_sh
        d *= 2
    hprev = hcar[...]
    if stride == 1:
        hfull = av * hprev + bv
    else:
        hfull = av * jnp.concatenate([hprev] * (tb // stride), axis=0) + bv
    hl = hfull[tb - stride:tb, :]
    hcar[...] = hl
    hlast_ref[0] = hl
    yabc_ref[0, :, 0:gw] = (hfull * _gelu(ga)).astype(BF16)

    u = _gelu(za[:, 2 * gw:3 * gw])
    vn = _layer_norm(_gelu(za[:, 3 * gw:4 * gw]), lng_ref[...], lnb_ref[...])
    if not prompt:
        vn_ref[0] = vn
    ghead = _lane_head((CHUNK, gw), gw // G_HEADS)
    for c in range(tb // CHUNK):
        crow = slice(c * CHUNK, (c + 1) * CHUNK)
        vnc = vn[crow].astype(BF16)
        mixed = bs_ref[...]
        for h in range(G_HEADS):
            mixed = mixed + jnp.where(ghead == h, _dot(ws_ref[h], vnc), 0.0)
        yabc_ref[0, crow, gw:2 * gw] = (u[crow] * mixed).astype(BF16)

    xp = za[:, 4 * gw:5 * gw]
    n0 = hp + tb
    pext[hp:n0, :] = xp
    s1, s2, s3, s4 = starts
    zero8 = jnp.zeros((max(s3, SUBLANES), gw), F32)
    l1[0:s1, :] = zero8[0:s1]
    l2[0:s2, :] = zero8[0:s2]
    l3[0:s3, :] = zero8[0:s3]
    l1[s1:n0, :] = pext[s1:n0, :] + pext[s1 - stride:n0 - stride, :]
    l2[s2:n0, :] = l1[s2:n0, :] + l1[s2 - 2 * stride:n0 - 2 * stride, :]
    l3[s3:n0, :] = l2[s3:n0, :] + l2[s3 - 4 * stride:n0 - 4 * stride, :]
    w16 = l3[hp:n0, :] + l3[hp - 8 * stride:n0 - 8 * stride, :]
    pgrp = _lane_head((tb, gw), gw // len(POOL_WINDOWS))
    sums = jnp.where(pgrp == 0, l1[hp:n0, :], jnp.where(pgrp == 1, l2[hp:n0, :],
                     jnp.where(pgrp == 2, l3[hp:n0, :], w16)))
    win = jnp.where(pgrp == 0, POOL_WINDOWS[0], jnp.where(pgrp == 1, POOL_WINDOWS[1],
                    jnp.where(pgrp == 2, POOL_WINDOWS[2], POOL_WINDOWS[3])))
    pos = start_pos + _div_pow2(tstep * tb + rows, stride)
    cnt = jnp.minimum(pos + 1, win).astype(F32)
    pooled = sums / cnt - xp
    yc = _dot(pooled.astype(BF16), poolw_ref[...]) * pscale_ref[...]
    yabc_ref[0, :, 2 * gw:3 * gw] = yc.astype(BF16)
    ptail = pext[n0 - hp:n0, :]
    ptail_ref[0] = ptail
    if tb >= hp:
        pext[0:hp, :] = ptail


def _block_diag(w):
    h, a, b = w.shape
    eye = jnp.eye(h, dtype=w.dtype)
    return (eye[:, None, :, None] * w[:, :, None, :]).reshape(h * a, h * b)


def _mixers(x, lw, conv0, pool0, h0, *, stride, tb, start_pos, prompt):
    nseq, rows, dm = x.shape
    gw = GROUP_W
    hc, hp, starts = _mixer_dims(stride)
    nt = rows // tb
    n0 = hp + tb
    kern = functools.partial(_mixers_kernel, stride=stride, tb=tb, start_pos=start_pos, prompt=prompt)

    def const(shape):
        return pl.BlockSpec(shape, lambda b, t: (0,) * len(shape))

    def per_seq(shape):
        return pl.BlockSpec((1,) + shape, lambda b, t: (b,) + (0,) * len(shape))

    def per_blk(width):
        return pl.BlockSpec((1, tb, width), lambda b, t: (b, t, 0))

    in_specs = [per_blk(dm), const(lw["w_abc"].shape), const(lw["w_qkv"].shape), const(lw["w_idx"].shape),
                const(lw["w_kt"].shape), const((SUBLANES, gw)), const((1, gw)), const((gw, 2 * gw)),
                const((1, 2 * gw)), const((1, gw)), const((1, gw)), const((1, gw)),
                const((G_HEADS, CHUNK, CHUNK)), const((CHUNK, gw)), const((gw, gw)), const((1, gw)),
                per_seq((hc, gw)), per_seq((hp, gw)), per_seq((stride, gw))]
    out_shape = [jax.ShapeDtypeStruct((nseq, rows, 3 * gw), BF16),
                 jax.ShapeDtypeStruct((nseq, rows, gw), BF16),
                 jax.ShapeDtypeStruct((nseq, rows, gw), F32),
                 jax.ShapeDtypeStruct((nseq, rows, gw), F32),
                 jax.ShapeDtypeStruct((nseq, rows, IDX_HEADS * IDX_DIM), BF16),
                 jax.ShapeDtypeStruct((nseq, rows, LANES), F32),
                 jax.ShapeDtypeStruct((nseq, hc, gw), F32),
                 jax.ShapeDtypeStruct((nseq, hp, gw), F32),
                 jax.ShapeDtypeStruct((nseq, stride, gw), F32)]
    out_specs = [per_blk(3 * gw), per_blk(gw), per_blk(gw), per_blk(gw), per_blk(IDX_HEADS * IDX_DIM),
                 per_blk(LANES), per_seq((hc, gw)), per_seq((hp, gw)), per_seq((stride, gw))]
    if prompt:
        nck = rows // KEY_CHUNK
        cpb = tb // KEY_CHUNK
        out_shape += [jax.ShapeDtypeStruct((nseq, nck, gw, KEY_CHUNK), BF16),
                      jax.ShapeDtypeStruct((nseq, nck, IDX_DIM, KEY_CHUNK), BF16),
                      jax.ShapeDtypeStruct((nseq, nck, ATT_HEADS * KEY_CHUNK, gw), BF16)]
        out_specs += [pl.BlockSpec((1, cpb, gw, KEY_CHUNK), lambda b, t: (b, t, 0, 0)),
                      pl.BlockSpec((1, cpb, IDX_DIM, KEY_CHUNK), lambda b, t: (b, t, 0, 0)),
                      pl.BlockSpec((1, cpb, ATT_HEADS * KEY_CHUNK, gw), lambda b, t: (b, t, 0, 0))]
    else:
        out_shape += [jax.ShapeDtypeStruct((nseq, rows, gw), F32)]
        out_specs += [per_blk(gw)]
    scratch = [pltpu.VMEM((hc + tb, gw), F32), pltpu.VMEM((n0, gw), F32), pltpu.VMEM((n0, gw), F32),
               pltpu.VMEM((n0, gw), F32), pltpu.VMEM((n0, gw), F32), pltpu.VMEM((stride, gw), F32)]
    return pl.pallas_call(
        kern, out_shape=out_shape, grid=(nseq, nt), in_specs=in_specs, out_specs=out_specs,
        scratch_shapes=scratch, name="mixers_prompt" if prompt else "mixers_sample",
        compiler_params=pltpu.CompilerParams(dimension_semantics=("arbitrary", "arbitrary"),
                                             vmem_limit_bytes=VMEM_LIMIT),
    )(x, lw["w_abc"], lw["w_qkv"], lw["w_idx"], lw["w_kt"], lw["conv_w"], lw["conv_b"], lw["w_gate"],
      lw["b_gate"], lw["lam"], lw["ln_g"], lw["ln_b"], lw["ws_p" if prompt else "ws_s"],
      lw["bs_p" if prompt else "bs_s"], lw["pool_w"], lw["pool_scale"], conv0, pool0, h0)


def _count_rows(acc):
    return jnp.sum(acc, axis=1, keepdims=True)


def _kth_largest_key(count_ge, shape, topk):
    def step(it, base):
        bit = lax.shift_left(jnp.int32(1), 31 - it)
        cand = jnp.where(it == 0, jnp.int32(0), base | bit)
        return jnp.where(count_ge(cand) >= topk, cand, base)
    return lax.fori_loop(0, 32, step, jnp.full(shape, INT_MIN, I32))


I16_MIN = -2 ** 15


def _kth_largest_half(count_ge, shape, kneed):
    def step(it, base):
        bit = lax.shift_left(jnp.int32(1), 15 - it)
        cand = jnp.where(it == 0, jnp.int32(0), base | bit)
        return jnp.where(count_ge(cand) >= kneed, cand, base)
    return lax.fori_loop(0, 16, step, jnp.full(shape, I16_MIN, I32))


def _upper_tri_ones():
    r = lax.broadcasted_iota(I32, (KEY_CHUNK, 2 * KEY_CHUNK), 0)
    c = lax.broadcasted_iota(I32, (KEY_CHUNK, 2 * KEY_CHUNK), 1)
    return jnp.where(r <= c, 1.0, 0.0).astype(BF16)


def _select_chunks(keys, thr, need, carry, tri):
    eqs = [(k == thr) & (k != INT_MIN) for k in keys]
    counts = [_dot(jnp.where(e, 1.0, 0.0).astype(BF16), tri) for e in eqs]
    sels = []
    for k, e, cnt in zip(keys, eqs, counts):
        sels.append((k > thr) | (e & (cnt[:, :KEY_CHUNK] + carry <= need)))
        carry = carry + cnt[:, KEY_CHUNK:]
    return sels, carry


def _dsa_prompt_kernel(b31_ref, qi_ref, kiwi_ref, q_ref, kit_ref, kt_ref, vbd_ref, btab_ref, out_ref,
                       qh_scr, wb_scr, qs_scr, s_scr, h_scr, m_scr, l_scr, acc_scr, *, topk, qb):
    i = pl.program_id(1)
    gw = GROUP_W

    qi = qi_ref[0]
    kiwi = kiwi_ref[0]
    for h in range(IDX_HEADS):
        qh_scr[h * qb:(h + 1) * qb, :] = qi[:, h * IDX_DIM:(h + 1) * IDX_DIM]
        wcol = kiwi[:, IDX_DIM + h:IDX_DIM + h + 1] * IDX_SCALE
        wb_scr[h] = jnp.broadcast_to(wcol, (qb, LANES))
    qq = q_ref[0]
    for h in range(ATT_HEADS):
        qs_scr[h] = qq[:, h * HEAD_DIM:(h + 1) * HEAD_DIM]

    row = lax.broadcasted_iota(I32, (qb, KEY_CHUNK), 0)
    col = lax.broadcasted_iota(I32, (qb, KEY_CHUNK), 1)

    g_diag = i // CHUNK_GROUP

    def score_group(g, last):
        kit = jnp.concatenate([kit_ref[0, g * CHUNK_GROUP + c] for c in range(CHUNK_GROUP)], axis=1)
        rr = _dot(qh_scr[...], kit)
        for c in range(CHUNK_GROUP):
            j = g * CHUNK_GROUP + c
            cols = slice(c * KEY_CHUNK, (c + 1) * KEY_CHUNK)
            s = jnp.zeros((qb, KEY_CHUNK), F32)
            for h in range(IDX_HEADS):
                s = s + wb_scr[h] * jnp.maximum(rr[h * qb:(h + 1) * qb, cols], 0.0)
            key = _score_key(s)
            if last:
                diag_key = jnp.where(col <= row, key, INT_MIN)
                key = jnp.where(j < i, key, jnp.where(j == i, diag_key, INT_MIN))
            s_scr[j] = key
            h_scr[j] = lax.shift_right_arithmetic(key, 16).astype(jnp.int16)

    def score_body(g, c):
        score_group(g, False)
        return c
    lax.fori_loop(0, g_diag, score_body, 0)
    score_group(g_diag, True)

    def count(pred):
        def body(g, acc):
            for c in range(CHUNK_GROUP):
                acc = acc + jnp.where(pred(h_scr[g * CHUNK_GROUP + c]), jnp.int16(1), jnp.int16(0))
            return acc
        acc = lax.fori_loop(0, g_diag + 1, body, jnp.zeros((qb, KEY_CHUNK), jnp.int16))
        return _count_rows(acc.astype(F32))

    def ge(cand):
        c16 = cand.astype(jnp.int16)
        return count(lambda x: x >= c16)

    def gt(cand):
        c16 = cand.astype(jnp.int16)
        return count(lambda x: x > c16)

    shape = (qb, KEY_CHUNK)
    t_hi = _kth_largest_half(ge, shape, topk)
    need_hi = topk - gt(t_hi)
    t_hi16 = t_hi.astype(jnp.int16)

    def low_body(g, c):
        for cc in range(CHUNK_GROUP):
            j = g * CHUNK_GROUP + cc
            low = ((s_scr[j] & 0xFFFF) + I16_MIN).astype(jnp.int16)
            h_scr[j] = jnp.where(h_scr[j] == t_hi16, low, jnp.int16(I16_MIN))
        return c
    lax.fori_loop(0, g_diag + 1, low_body, 0)
    t_lo = _kth_largest_half(ge, shape, need_hi)
    need = jnp.broadcast_to(need_hi - gt(t_lo), shape)
    thr = lax.shift_left(t_hi, 16) | (t_lo - I16_MIN)

    m_scr[...] = jnp.full(m_scr.shape, NEG_BIG, F32)
    l_scr[...] = jnp.zeros(l_scr.shape, F32)
    acc_scr[...] = jnp.zeros(acc_scr.shape, F32)
    tri = _upper_tri_ones()
    g_far = jnp.maximum(i - 1, 0) // CHUNK_GROUP

    def logits(j, h):
        return _dot(qs_scr[h], kt_ref[0, j, pl.ds(h * HEAD_DIM, HEAD_DIM), :])

    def near_bias(j, h):
        tab = btab_ref[h, jnp.clip(j - (i - 1), 0, 1)]
        return jnp.where(j >= i - 1, tab - b31_ref[h], 0.0)

    def max_group(g, carry, near):
        js = [g * CHUNK_GROUP + c for c in range(CHUNK_GROUP)]
        sels, carry = _select_chunks([s_scr[j] for j in js], thr, need, carry, tri)
        vmax = [None] * ATT_HEADS
        for c, j in enumerate(js):
            madd = jnp.where(sels[c], 0.0, NEG_BIG)
            s_scr[j] = lax.bitcast_convert_type(madd, I32)
            for h in range(ATT_HEADS):
                v = logits(j, h) + madd
                if near:
                    v = v + near_bias(j, h)
                vmax[h] = v if vmax[h] is None else jnp.maximum(vmax[h], v)
        for h in range(ATT_HEADS):
            m_scr[h] = jnp.maximum(m_scr[h], vmax[h])
        return carry

    carry = lax.fori_loop(0, g_far, lambda g, c: max_group(g, c, False), jnp.zeros((qb, KEY_CHUNK), F32))
    lax.fori_loop(g_far, g_diag + 1, lambda g, c: max_group(g, c, True), carry)
    for h in range(ATT_HEADS):
        m_row = jnp.max(m_scr[h], axis=1, keepdims=True)
        m_scr[h] = jnp.broadcast_to(-m_row, (qb, LANES))

    def exp_group(g, near):
        pv = None
        lsum = [None] * ATT_HEADS
        for c in range(CHUNK_GROUP):
            j = g * CHUNK_GROUP + c
            madd = lax.bitcast_convert_type(s_scr[j], F32)
            ps = []
            for h in range(ATT_HEADS):
                shift = m_scr[h] + near_bias(j, h) if near else m_scr[h]
                p = jnp.exp(logits(j, h) + (madd + shift))
                lsum[h] = p if lsum[h] is None else lsum[h] + p
                ps.append(p.astype(BF16))
            d = _dot(jnp.concatenate(ps, axis=1), vbd_ref[0, j])
            pv = d if pv is None else pv + d
        for h in range(ATT_HEADS):
            l_scr[h] = l_scr[h] + lsum[h]
        acc_scr[...] = acc_scr[...] + pv

    def far_body(g, c):
        exp_group(g, False)
        return c

    def near_body(g, c):
        exp_group(g, True)
        return c
    lax.fori_loop(0, g_far, far_body, 0)
    lax.fori_loop(g_far, g_diag + 1, near_body, 0)
    lane64 = lax.broadcasted_iota(I32, (qb, LANES), 1) < HEAD_DIM
    ls = [jnp.broadcast_to(jnp.sum(l_scr[h], axis=1, keepdims=True), (qb, LANES)) for h in range(ATT_HEADS)]
    l_cat = jnp.concatenate([jnp.where(lane64, ls[0], ls[1]), jnp.where(lane64, ls[2], ls[3])], axis=1)
    out_ref[0] = (acc_scr[...] / l_cat).astype(BF16)


def _dsa_prompt(qi, kiwi, q, kit, kt, vbd, btab, b31, *, topk):
    nb, t, _ = q.shape
    qb = KEY_CHUNK
    nq = t // qb
    gw = GROUP_W
    kern = functools.partial(_dsa_prompt_kernel, topk=topk, qb=qb)

    def per_q(width):
        return pl.BlockSpec((1, qb, width), lambda b, i: (b, i, 0))

    def per_batch(shape):
        return pl.BlockSpec((1,) + shape, lambda b, i: (b,) + (0,) * len(shape))

    in_specs = [pl.BlockSpec(memory_space=pltpu.SMEM),
                per_q(IDX_HEADS * IDX_DIM), per_q(LANES), per_q(gw),
                per_batch((nq, IDX_DIM, KEY_CHUNK)), per_batch((nq, gw, KEY_CHUNK)),
                pl.BlockSpec((1, nq, ATT_HEADS * KEY_CHUNK, gw), lambda b, i: (b, 0, 0, 0),
                             pipeline_mode=pl.Buffered(1)),
                pl.BlockSpec((ATT_HEADS, 2, KEY_CHUNK, KEY_CHUNK), lambda b, i: (0, 0, 0, 0))]
    scratch = [pltpu.VMEM((IDX_HEADS * qb, IDX_DIM), BF16), pltpu.VMEM((IDX_HEADS, qb, LANES), F32),
               pltpu.VMEM((ATT_HEADS, qb, HEAD_DIM), BF16), pltpu.VMEM((nq, qb, KEY_CHUNK), I32),
               pltpu.VMEM((nq, qb, KEY_CHUNK), jnp.int16),
               pltpu.VMEM((ATT_HEADS, qb, LANES), F32), pltpu.VMEM((ATT_HEADS, qb, LANES), F32),
               pltpu.VMEM((qb, gw), F32)]
    return pl.pallas_call(
        kern, out_shape=jax.ShapeDtypeStruct((nb, t, gw), BF16), grid=(nb, nq),
        in_specs=in_specs, out_specs=per_q(gw), scratch_shapes=scratch, name="dsa_prompt",
        compiler_params=pltpu.CompilerParams(dimension_semantics=("arbitrary", "arbitrary"),
                                             vmem_limit_bytes=VMEM_LIMIT),
    )(b31, qi, kiwi, q, kit, kt, vbd, btab)


PAGES_PER_STEP = 8
SAMPLE_COUNT_UNROLL = 8


def _sample_scores(qi, wpad, keys_t):
    rr = _dot(qi, keys_t)
    s = jnp.zeros((SUBLANES, rr.shape[1]), F32)
    for h in range(IDX_HEADS):
        rows = slice(h * SUBLANES, (h + 1) * SUBLANES)
        s = s + wpad[rows, 0:1] * jnp.maximum(rr[rows], 0.0)
    return _score_key(s)


def _dsa_sample_index_kernel(pt_ref, qi_ref, wpad_ref, *refs, npages, topk):
    g = PAGES_PER_STEP
    page_refs = refs[:g]
    kinew_ref, spast_ref, snew_ref, thr_ref, need_ref = refs[g:]
    s = pl.program_id(1)
    qi = qi_ref[0]
    wpad = wpad_ref[0]
    pages = jnp.concatenate([page_refs[p][0, 0].astype(BF16) for p in range(g)], axis=1)
    keys = _sample_scores(qi, wpad, pages)
    for p in range(g):
        spast_ref[0, s * g + p] = keys[:, p * KEY_CHUNK:(p + 1) * KEY_CHUNK]

    @pl.when(s == pl.num_programs(1) - 1)
    def _():
        row = lax.broadcasted_iota(I32, (SUBLANES, KEY_CHUNK), 0)
        col = lax.broadcasted_iota(I32, (SUBLANES, KEY_CHUNK), 1)
        knew = jnp.where(col <= row, _sample_scores(qi, wpad, kinew_ref[0]), INT_MIN)
        snew_ref[0] = knew

        def count(pred):
            def body(jj, acc):
                for c in range(SAMPLE_COUNT_UNROLL):
                    acc = acc + jnp.where(pred(spast_ref[0, jj * SAMPLE_COUNT_UNROLL + c]), 1.0, 0.0)
                return acc
            acc = lax.fori_loop(0, npages // SAMPLE_COUNT_UNROLL, body, jnp.where(pred(knew), 1.0, 0.0))
            return _count_rows(acc)

        thr = _kth_largest_key(lambda cand: count(lambda k: k >= cand), (SUBLANES, KEY_CHUNK), topk)
        thr_ref[0] = thr
        need_ref[0] = jnp.broadcast_to(topk - count(lambda k: k > thr), (SUBLANES, KEY_CHUNK))


def _dsa_sample_attend_kernel(pt_ref, qbd_ref, spast_ref, snew_ref, thr_ref, need_ref, *refs, npages):
    g = PAGES_PER_STEP
    k_refs = refs[:g]
    v_refs = refs[g:2 * g]
    knew_ref, vnew_ref, tabs_ref, out_ref, m_scr, l_scr, acc_scr, carry_scr = refs[2 * g:]
    s = pl.program_id(1)
    nh = ATT_HEADS * SUBLANES

    @pl.when(s == 0)
    def _():
        m_scr[...] = jnp.full(m_scr.shape, NEG_BIG, F32)
        l_scr[...] = jnp.zeros(l_scr.shape, F32)
        acc_scr[...] = jnp.zeros(acc_scr.shape, F32)
        carry_scr[...] = jnp.zeros(carry_scr.shape, F32)

    qbd = qbd_ref[0]
    thr = thr_ref[0]
    need = need_ref[0]
    tri = _upper_tri_ones()

    def select(keys):
        sels, carry = _select_chunks(keys, thr, need, carry_scr[...], tri)
        carry_scr[...] = carry
        return [jnp.concatenate([jnp.where(sel8, 1, 0)] * ATT_HEADS, axis=0) for sel8 in sels]

    def attend(sel, kt, vt, bias):
        lg = jnp.where(sel > 0, _dot(qbd, kt) + bias, NEG_BIG)
        m_old = m_scr[...]
        m_new = jnp.maximum(m_old, jnp.max(lg, axis=1, keepdims=True))
        alpha = jnp.exp(m_old - m_new)
        p = jnp.where(sel > 0, jnp.exp(lg - m_new[:, 0:1]), 0.0)
        l_scr[...] = alpha * l_scr[...] + jnp.sum(p, axis=1, keepdims=True)
        m_scr[...] = m_new
        alpha2 = jnp.concatenate([alpha, alpha], axis=1)
        acc_scr[...] = acc_scr[...] * alpha2 + _dot_nt(p.astype(BF16), vt)

    biases = [jnp.where(s * g + p == npages - 1, tabs_ref[0], tabs_ref[2]) for p in range(g)]
    attend(jnp.concatenate(select([spast_ref[0, p] for p in range(g)]), axis=1),
           jnp.concatenate([k_refs[p][0, 0].astype(BF16) for p in range(g)], axis=1),
           jnp.concatenate([v_refs[p][0, 0].astype(BF16) for p in range(g)], axis=1),
           jnp.concatenate(biases, axis=1))

    @pl.when(s == pl.num_programs(1) - 1)
    def _():
        attend(select([snew_ref[0]])[0], knew_ref[0], vnew_ref[0], tabs_ref[1])
        l2 = jnp.concatenate([l_scr[...], l_scr[...]], axis=1)
        o = acc_scr[...] / l2
        head = _lane_head((SUBLANES, GROUP_W), HEAD_DIM)
        out = jnp.zeros((SUBLANES, GROUP_W), F32)
        for h in range(ATT_HEADS):
            out = out + jnp.where(head == h, o[h * SUBLANES:(h + 1) * SUBLANES], 0.0)
        out_ref[0] = out


def _dsa_sample(layer, page_table, qi_pad, wpad, kinew, qbd, knew, vnew, cache_ik, cache_k, cache_v, tabs, *, topk):
    db, npages = page_table.shape
    g = PAGES_PER_STEP
    ns = npages // g
    gw = GROUP_W
    pt = page_table.reshape(-1)

    def page_spec(width, p):
        return pl.BlockSpec((1, 1, width, PAGE_SIZE),
                            lambda b, s, pt_ref: (layer, pt_ref[b * npages + s * g + p], 0, 0))

    def per_b(shape):
        return pl.BlockSpec((1,) + shape, lambda b, s, pt_ref: (b,) + (0,) * len(shape))

    index_kern = functools.partial(_dsa_sample_index_kernel, npages=npages, topk=topk)
    spast, snew, thr, need = pl.pallas_call(
        index_kern,
        out_shape=(jax.ShapeDtypeStruct((db, npages, SUBLANES, KEY_CHUNK), I32),
                   jax.ShapeDtypeStruct((db, SUBLANES, KEY_CHUNK), I32),
                   jax.ShapeDtypeStruct((db, SUBLANES, KEY_CHUNK), I32),
                   jax.ShapeDtypeStruct((db, SUBLANES, KEY_CHUNK), F32)),
        grid_spec=pltpu.PrefetchScalarGridSpec(
            num_scalar_prefetch=1, grid=(db, ns),
            in_specs=[per_b((IDX_HEADS * SUBLANES, IDX_DIM)), per_b((IDX_HEADS * SUBLANES, LANES))]
                     + [page_spec(IDX_DIM, p) for p in range(g)] + [per_b((IDX_DIM, KEY_CHUNK))],
            out_specs=[per_b((npages, SUBLANES, KEY_CHUNK)), per_b((SUBLANES, KEY_CHUNK)),
                       per_b((SUBLANES, KEY_CHUNK)), per_b((SUBLANES, KEY_CHUNK))]),
        name="dsa_sample_index",
        compiler_params=pltpu.CompilerParams(dimension_semantics=("arbitrary", "arbitrary"),
                                             vmem_limit_bytes=VMEM_LIMIT),
    )(pt, qi_pad, wpad, *([cache_ik] * g), kinew)

    attend_kern = functools.partial(_dsa_sample_attend_kernel, npages=npages)
    nh = ATT_HEADS * SUBLANES
    return pl.pallas_call(
        attend_kern,
        out_shape=jax.ShapeDtypeStruct((db, SUBLANES, gw), F32),
        grid_spec=pltpu.PrefetchScalarGridSpec(
            num_scalar_prefetch=1, grid=(db, ns),
            in_specs=[per_b((nh, gw)),
                      pl.BlockSpec((1, g, SUBLANES, KEY_CHUNK), lambda b, s, pt_ref: (b, s, 0, 0)),
                      per_b((SUBLANES, KEY_CHUNK)), per_b((SUBLANES, KEY_CHUNK)), per_b((SUBLANES, KEY_CHUNK))]
                     + [page_spec(gw, p) for p in range(g)] + [page_spec(gw, p) for p in range(g)]
                     + [per_b((gw, KEY_CHUNK)), per_b((gw, KEY_CHUNK)),
                        pl.BlockSpec((3, nh, KEY_CHUNK), lambda b, s, pt_ref: (0, 0, 0))],
            out_specs=per_b((SUBLANES, gw)),
            scratch_shapes=[pltpu.VMEM((nh, LANES), F32), pltpu.VMEM((nh, LANES), F32),
                            pltpu.VMEM((nh, gw), F32), pltpu.VMEM((SUBLANES, KEY_CHUNK), F32)]),
        name="dsa_sample_attend",
        compiler_params=pltpu.CompilerParams(dimension_semantics=("arbitrary", "arbitrary"),
                                             vmem_limit_bytes=VMEM_LIMIT),
    )(pt, qbd, spast, snew, thr, need, *([cache_k] * g), *([cache_v] * g), knew, vnew, tabs)


FF_CHUNK = 1024


def _ffn_kernel(x_ref, yabc_ref, yd_ref, woa_ref, wod_ref, g1_ref, b1_ref, w1_ref, w2_ref, g2_ref, b2_ref,
                out_ref, *, alpha):
    x = x_ref[...]
    mix = _dot(yabc_ref[...], woa_ref[...]) + _dot(yd_ref[...], wod_ref[...])
    x1 = _layer_norm(alpha * x + mix, g1_ref[...], b1_ref[...])
    x1b = x1.astype(BF16)
    acc = jnp.zeros(x.shape, F32)
    for c in range(w1_ref.shape[1] // FF_CHUNK):
        cols = slice(c * FF_CHUNK, (c + 1) * FF_CHUNK)
        hdn = jnp.square(jnp.maximum(_dot(x1b, w1_ref[:, cols]), 0.0))
        acc = acc + _dot(hdn.astype(BF16), w2_ref[cols, :])
    out_ref[...] = _layer_norm(alpha * x1 + acc, g2_ref[...], b2_ref[...])


def _ffn(x, yabc, yd, lw, *, tm, alpha):
    rows, dm = x.shape
    dff = lw["w1"].shape[1]

    def blk(width):
        return pl.BlockSpec((tm, width), lambda r: (r, 0))

    def const(shape):
        return pl.BlockSpec(shape, lambda r: (0,) * len(shape), pipeline_mode=pl.Buffered(1))

    return pl.pallas_call(
        functools.partial(_ffn_kernel, alpha=alpha),
        out_shape=jax.ShapeDtypeStruct((rows, dm), F32), grid=(rows // tm,),
        in_specs=[blk(dm), blk(3 * GROUP_W), blk(GROUP_W), const((3 * GROUP_W, dm)), const((GROUP_W, dm)),
                  const((1, dm)), const((1, dm)), const((dm, dff)), const((dff, dm)), const((1, dm)),
                  const((1, dm))],
        out_specs=blk(dm), name="ffn",
        compiler_params=pltpu.CompilerParams(dimension_semantics=("arbitrary",),
                                             vmem_limit_bytes=VMEM_LIMIT),
    )(x, yabc, yd, lw["w_out_abc"], lw["w_out_d"], lw["ln1_g"], lw["ln1_b"], lw["w1"], lw["w2"],
      lw["ln2_g"], lw["ln2_b"])


def _layer_weights(l, dec_batch, dec_seq, w_in, conv_w, conv_b, lru_wa, lru_ba, lru_wx, lru_bx, lru_lambda,
                   gmlp_ln_g, gmlp_ln_b, gmlp_ws, gmlp_bs, pool_w, pool_scale, w_out, ln1_g, ln1_b, ln2_g,
                   ln2_b, ffn_w1, ffn_w2):
    gw = GROUP_W
    dm = w_in.shape[1]
    wi = w_in[l]
    o_q, o_k, o_v = 5 * gw, 6 * gw, 7 * gw
    o_qi = 8 * gw
    o_ki = o_qi + IDX_HEADS * IDX_DIM
    o_wi = o_ki + IDX_DIM
    pad = jnp.zeros((dm, LANES - IDX_DIM - IDX_HEADS), wi.dtype)
    row = lambda a: a[l].reshape(1, -1)
    tril = jnp.tril(jnp.ones((CHUNK, CHUNK), F32))
    ws_tril = gmlp_ws[l] * tril
    eye_b = jnp.eye(dec_batch, dtype=F32)
    ws_small = ws_tril[:, :dec_seq, :dec_seq]
    ws_s = (ws_small[:, :, None, :, None] * eye_b[None, None, :, None, :]).reshape(
        G_HEADS, dec_seq * dec_batch, dec_seq * dec_batch)
    bs_cols = jnp.repeat(gmlp_bs[l].T, gw // G_HEADS, axis=1)
    return {
        "w_abc": wi[:, :o_q].astype(BF16),
        "w_qkv": wi[:, o_q:o_qi].astype(BF16),
        "w_idx": jnp.concatenate([wi[:, o_qi:], pad], axis=1).astype(BF16),
        "w_kt": jnp.concatenate([wi[:, o_k:o_v], wi[:, o_ki:o_wi]], axis=1).T.astype(BF16),
        "conv_w": jnp.concatenate([conv_w[l], jnp.zeros((SUBLANES - CONV_W, gw), F32)], axis=0),
        "conv_b": row(conv_b),
        "w_gate": jnp.concatenate([_block_diag(lru_wa[l]), _block_diag(lru_wx[l])], axis=1).astype(BF16),
        "b_gate": jnp.concatenate([lru_ba[l], lru_bx[l]]).reshape(1, -1),
        "lam": row(lru_lambda), "ln_g": row(gmlp_ln_g), "ln_b": row(gmlp_ln_b),
        "ws_p": ws_tril.astype(BF16), "ws_s": ws_s.astype(BF16),
        "bs_p": bs_cols, "bs_s": jnp.repeat(bs_cols[:dec_seq], dec_batch, axis=0),
        "pool_w": _block_diag(pool_w[l]).astype(BF16), "pool_scale": row(pool_scale),
        "w_out_abc": w_out[l, :3 * gw].astype(BF16), "w_out_d": w_out[l, 3 * gw:].astype(BF16),
        "ln1_g": row(ln1_g), "ln1_b": row(ln1_b), "ln2_g": row(ln2_g), "ln2_b": row(ln2_b),
        "w1": ffn_w1[l].astype(BF16), "w2": ffn_w2[l].astype(BF16),
    }


PROMPT_TB = 256
FFN_TM = 512


def kernel(x_prompt, x_sample, cache_k, cache_v, cache_idx_k, state_lru_h, state_conv, state_pool, page_table,
           w_in, conv_w, conv_b, lru_wa, lru_ba, lru_wx, lru_bx, lru_lambda, gmlp_ln_g, gmlp_ln_b, gmlp_ws,
           gmlp_bs, pool_w, pool_scale, rel_bias, w_out, ln1_g, ln1_b, ln2_g, ln2_b, ffn_w1, ffn_w2):
    nb, t, dm = x_prompt.shape
    db, ts, _ = x_sample.shape
    depth = w_in.shape[0]
    gw = GROUP_W
    npages = page_table.shape[1]
    past_len = npages * PAGE_SIZE
    topk_p = min(TOPK_MAX, t // 4)
    topk_s = min(TOPK_MAX, (past_len + ts) // 4)
    alpha = (2 * depth) ** 0.25
    n_pool = cache_k.shape[1]
    cache_kt = jnp.transpose(cache_k, (0, 1, 3, 4, 2)).reshape(depth, n_pool, gw, PAGE_SIZE)
    cache_vt = jnp.transpose(cache_v, (0, 1, 3, 4, 2)).reshape(depth, n_pool, gw, PAGE_SIZE)
    cache_ikt = jnp.swapaxes(cache_idx_k, 2, 3)

    btab, tabs = _bias_tables(rel_bias)
    b31 = rel_bias[NUM_BUCKETS - 1]
    hc_p, hp_p, _ = _mixer_dims(1)
    hc_s, hp_s, _ = _mixer_dims(db)
    srows = ts * db

    def to_rows(a):
        return jnp.swapaxes(a, 0, 1).reshape((srows,) + a.shape[2:])

    def from_rows(a):
        return jnp.swapaxes(a.reshape((ts, db) + a.shape[1:]), 0, 1)

    xp = x_prompt
    xs = to_rows(x_sample)[None]
    st_p, st_s = [], []
    for l in range(depth):
        lw = _layer_weights(l, db, ts, w_in, conv_w, conv_b, lru_wa, lru_ba, lru_wx, lru_bx, lru_lambda,
                            gmlp_ln_g, gmlp_ln_b, gmlp_ws, gmlp_bs, pool_w, pool_scale, w_out, ln1_g, ln1_b,
                            ln2_g, ln2_b, ffn_w1, ffn_w2)
        (yabc, q, k, v, qi, kiwi, ctail, ptail, hlast, kt, kit, vbd) = _mixers(
            xp, lw, jnp.zeros((nb, hc_p, gw), F32), jnp.zeros((nb, hp_p, gw), F32), jnp.zeros((nb, 1, gw), F32),
            stride=1, tb=PROMPT_TB, start_pos=0, prompt=True)
        yd = _dsa_prompt(qi, kiwi, q, kit, kt, vbd, btab, b31, topk=topk_p)
        xp = _ffn(xp.reshape(nb * t, dm), yabc.reshape(nb * t, 3 * gw), yd.reshape(nb * t, gw), lw,
                  tm=FFN_TM, alpha=alpha).reshape(nb, t, dm)
        st_p.append((k.reshape(nb, t, ATT_HEADS, HEAD_DIM), v.reshape(nb, t, ATT_HEADS, HEAD_DIM),
                     kiwi[:, :, :IDX_DIM], hlast[:, 0], ctail[:, hc_p - (CONV_W - 1):],
                     ptail[:, hp_p - (POOL_MAX - 1):]))
        conv0 = jnp.swapaxes(state_conv[l], 0, 1).reshape(1, hc_s, gw)
        pool0 = jnp.swapaxes(state_pool[l], 0, 1).reshape(1, hp_s, gw)
        (yabc_s, q_s, k_s, v_s, qi_s, kiwi_s, ctail_s, ptail_s, hlast_s, vn_s) = _mixers(
            xs, lw, conv0, pool0, state_lru_h[l][None], stride=db, tb=srows, start_pos=past_len, prompt=False)
        k_new = from_rows(k_s[0])
        v_new = from_rows(v_s[0])
        ki_new = from_rows(kiwi_s[0, :, :IDX_DIM])
        w_new = from_rows(kiwi_s[0, :, IDX_DIM:IDX_DIM + IDX_HEADS])
        qi_new = from_rows(qi_s[0]).reshape(db, ts, IDX_HEADS, IDX_DIM)
        q_new = from_rows(q_s[0]).reshape(db, ts, ATT_HEADS, HEAD_DIM)
        tpad = SUBLANES - ts
        qi_pad = jnp.pad(jnp.swapaxes(qi_new, 1, 2), ((0, 0), (0, 0), (0, tpad), (0, 0))).reshape(
            db, IDX_HEADS * SUBLANES, IDX_DIM)
        wpad = jnp.pad(jnp.swapaxes(w_new, 1, 2) * IDX_SCALE, ((0, 0), (0, 0), (0, tpad))).reshape(
            db, IDX_HEADS * SUBLANES, 1)
        wpad = jnp.broadcast_to(wpad, (db, IDX_HEADS * SUBLANES, LANES))
        q_heads = jnp.pad(jnp.swapaxes(q_new, 1, 2), ((0, 0), (0, 0), (0, tpad), (0, 0)))
        eye_h = jnp.eye(ATT_HEADS, dtype=q_heads.dtype)
        qbd = (q_heads[:, :, :, None, :] * eye_h[None, :, None, :, None]).reshape(
            db, ATT_HEADS * SUBLANES, gw)
        new_t = lambda a: jnp.swapaxes(jnp.pad(a, ((0, 0), (0, KEY_CHUNK - ts), (0, 0))), 1, 2).astype(BF16)
        yd_s = _dsa_sample(l, page_table, qi_pad, wpad, new_t(ki_new), qbd, new_t(k_new), new_t(v_new),
                           cache_ikt, cache_kt, cache_vt, tabs, topk=topk_s)
        yd_rows = to_rows(yd_s[:, :ts]).astype(BF16)
        xs = _ffn(xs[0], yabc_s[0], yd_rows, lw, tm=srows, alpha=alpha)[None]
        st_s.append((k_new.reshape(db, ts, ATT_HEADS, HEAD_DIM), v_new.reshape(db, ts, ATT_HEADS, HEAD_DIM),
                     ki_new, hlast_s[0],
                     jnp.swapaxes(ctail_s[0].reshape(CONV_W - 1, db, gw), 0, 1),
                     jnp.swapaxes(ptail_s[0].reshape(POOL_MAX - 1, db, gw), 0, 1),
                     from_rows(vn_s[0])))

    stack = lambda sts, j: jnp.stack([s[j] for s in sts], axis=0)
    return (xp, from_rows(xs[0]),
            stack(st_p, 0), stack(st_p, 1), stack(st_p, 2), stack(st_p, 3), stack(st_p, 4), stack(st_p, 5),
            stack(st_s, 0), stack(st_s, 1), stack(st_s, 2), stack(st_s, 3), stack(st_s, 4), stack(st_s, 5),
            stack(st_s, 6))
```

```python
import functools
import math

import jax
import jax.numpy as jnp
from jax import lax
from jax.experimental import pallas as pl
from jax.experimental.pallas import tpu as pltpu

F32 = jnp.float32
BF16 = jnp.bfloat16
I32 = jnp.int32

GROUP_W = 256
LRU_HEADS = 4
CONV_W = 4
RG_C = 8.0
CHUNK = 128
G_HEADS = 4
POOL_WINDOWS = (2, 4, 8, 16)
POOL_MAX = 16
ATT_HEADS = 4
HEAD_DIM = 64
IDX_HEADS = 8
IDX_DIM = 64
TOPK_MAX = 256
NUM_BUCKETS = 32
MAX_DISTANCE = 128
PAGE_SIZE = 128
LN_EPS = 1e-5

LANES = 128
SUBLANES = 8
KEY_CHUNK = 128
CHUNK_GROUP = 8
VMEM_LIMIT = 56 * 1024 * 1024

INT_MIN = -2 ** 31
NEG_BIG = -1e30
IDX_SCALE = (IDX_DIM ** -0.5) * (IDX_HEADS ** -0.5)
ATT_SCALE = HEAD_DIM ** -0.5


def _round_up(n, m):
    return (n + m - 1) // m * m


def _gelu(x):
    return x * (0.5 * (1.0 + jnp.tanh(math.sqrt(2.0 / math.pi) * (x + 0.044715 * (x * x * x)))))


def _layer_norm(x, g, b):
    mu = jnp.mean(x, axis=-1, keepdims=True)
    var = jnp.mean(jnp.square(x - mu), axis=-1, keepdims=True)
    return (x - mu) * lax.rsqrt(var + LN_EPS) * g + b


def _dot(a, b):
    return jnp.dot(a, b, preferred_element_type=F32)


def _dot_nt(a, b):
    return lax.dot_general(a, b, (((1,), (1,)), ((), ())), preferred_element_type=F32)


def _score_key(s):
    bits = lax.bitcast_convert_type(s, I32)
    bits = jnp.where(bits == INT_MIN, 0, bits)
    return bits ^ (lax.shift_right_arithmetic(bits, 31) & 0x7FFFFFFF)


def _div_pow2(x, d):
    shift = d.bit_length() - 1
    assert 1 << shift == d
    return lax.shift_right_logical(x, shift) if shift else x


def _lane_head(shape, width):
    return _div_pow2(lax.broadcasted_iota(I32, shape, len(shape) - 1), width)


def _bias_kernel(rel_ref, tabp_ref, tabs_ref):
    def bucket(dist):
        max_exact = NUM_BUCKETS // 2
        n = jnp.maximum(dist, 0)
        nf = jnp.maximum(n, 1).astype(F32)
        large = max_exact + (jnp.log(nf / max_exact) / math.log(MAX_DISTANCE / max_exact)
                             * (NUM_BUCKETS - max_exact)).astype(I32)
        large = jnp.minimum(large, NUM_BUCKETS - 1)
        return jnp.where(n < max_exact, n, large)

    def lookup(bk, h):
        acc = jnp.full(bk.shape, rel_ref[h], F32)
        for j in range(1, NUM_BUCKETS):
            acc = jnp.where(bk >= j, rel_ref[j * ATT_HEADS + h], acc)
        return acc

    r = lax.broadcasted_iota(I32, (KEY_CHUNK, KEY_CHUNK), 0)
    c = lax.broadcasted_iota(I32, (KEY_CHUNK, KEY_CHUNK), 1)
    b_prev = bucket(KEY_CHUNK + r - c)
    b_diag = bucket(r - c)
    t = lax.broadcasted_iota(I32, (SUBLANES, KEY_CHUNK), 0)
    cs = lax.broadcasted_iota(I32, (SUBLANES, KEY_CHUNK), 1)
    b_last = bucket(PAGE_SIZE + t - cs)
    b_new = bucket(t - cs)
    for h in range(ATT_HEADS):
        tabp_ref[h, 0] = lookup(b_prev, h)
        tabp_ref[h, 1] = lookup(b_diag, h)
        rows = pl.ds(h * SUBLANES, SUBLANES)
        tabs_ref[0, rows, :] = lookup(b_last, h)
        tabs_ref[1, rows, :] = lookup(b_new, h)
        tabs_ref[2, rows, :] = jnp.full((SUBLANES, KEY_CHUNK), rel_ref[(NUM_BUCKETS - 1) * ATT_HEADS + h], F32)


def _bias_tables(rel_bias):
    return pl.pallas_call(
        _bias_kernel,
        out_shape=(jax.ShapeDtypeStruct((ATT_HEADS, 2, KEY_CHUNK, KEY_CHUNK), F32),
                   jax.ShapeDtypeStruct((3, ATT_HEADS * SUBLANES, KEY_CHUNK), F32)),
        in_specs=[pl.BlockSpec(memory_space=pltpu.SMEM)],
        name="bias_tables",
    )(rel_bias.reshape(-1))


def _mixer_dims(stride):
    hc = _round_up((CONV_W - 1) * stride, SUBLANES)
    if stride % SUBLANES == 0:
        starts = tuple((w - 1) * stride for w in POOL_WINDOWS)
        hp = (POOL_MAX - 1) * stride
    else:
        starts = (SUBLANES,) * 4
        hp = _round_up(POOL_MAX - 1, SUBLANES) + SUBLANES
    return hc, hp, starts


def _mixers_kernel(*refs, stride, tb, start_pos, prompt):
    (x_ref, wabc_ref, wqkv_ref, widx_ref, wkt_ref, convw_ref, convb_ref, wgate_ref, bgate_ref, lam_ref,
     lng_ref, lnb_ref, ws_ref, bs_ref, poolw_ref, pscale_ref, conv0_ref, pool0_ref, h0_ref) = refs[:19]
    outs = refs[19:]
    if prompt:
        (yabc_ref, q_ref, k_ref, v_ref, qi_ref, kiwi_ref, ctail_ref, ptail_ref, hlast_ref,
         kt_ref, kit_ref, vbd_ref, cext, pext, l1, l2, l3, hcar) = outs
    else:
        (yabc_ref, q_ref, k_ref, v_ref, qi_ref, kiwi_ref, ctail_ref, ptail_ref, hlast_ref,
         vn_ref, cext, pext, l1, l2, l3, hcar) = outs
    hc, hp, starts = _mixer_dims(stride)
    tstep = pl.program_id(1)
    gw = GROUP_W

    @pl.when(tstep == 0)
    def _():
        cext[0:hc, :] = conv0_ref[0]
        pext[0:hp, :] = pool0_ref[0]
        hcar[...] = h0_ref[0]

    xb = x_ref[0].astype(BF16)
    za = _dot(xb, wabc_ref[...])
    zq = _dot(xb, wqkv_ref[...])
    zi = _dot(xb, widx_ref[...])
    q_ref[0] = (zq[:, 0:gw] * ATT_SCALE).astype(BF16)
    k_ref[0] = zq[:, gw:2 * gw]
    vv = zq[:, 2 * gw:3 * gw]
    v_ref[0] = vv
    qi_ref[0] = zi[:, 0:IDX_HEADS * IDX_DIM].astype(BF16)
    kiwi_ref[0] = zi[:, IDX_HEADS * IDX_DIM:]
    if prompt:
        zt = _dot_nt(wkt_ref[...], xb)
        vhead = _lane_head((KEY_CHUNK, gw), HEAD_DIM)
        for c in range(tb // KEY_CHUNK):
            cols = slice(c * KEY_CHUNK, (c + 1) * KEY_CHUNK)
            kt_ref[0, c] = zt[0:gw, cols].astype(BF16)
            kit_ref[0, c] = zt[gw:gw + IDX_DIM, cols].astype(BF16)
            for h in range(ATT_HEADS):
                vbd_ref[0, c, h * KEY_CHUNK:(h + 1) * KEY_CHUNK, :] = jnp.where(
                    vhead == h, vv[cols], 0.0).astype(BF16)

    xa = za[:, 0:gw]
    ga = za[:, gw:2 * gw]
    cext[hc:hc + tb, :] = xa
    xc = convb_ref[...] + jnp.zeros((tb, gw), F32)
    for j in range(CONV_W):
        off = hc + (j - (CONV_W - 1)) * stride
        xc = xc + cext[off:off + tb, :] * convw_ref[j:j + 1, :]
    ctail = cext[tb:tb + hc, :]
    ctail_ref[0] = ctail
    cext[0:hc, :] = ctail
    gates = _dot(xc.astype(BF16), wgate_ref[...]) + bgate_ref[...]
    r = jax.nn.sigmoid(gates[:, 0:gw])
    ig = jax.nn.sigmoid(gates[:, gw:2 * gw])
    nl = -lam_ref[...]
    softplus = jnp.maximum(nl, 0.0) + jnp.log1p(jnp.exp(-jnp.abs(nl)))
    log_a = -RG_C * r * softplus
    a = jnp.exp(log_a)
    mult = jnp.sqrt(-jnp.tanh(log_a) * (a * a + 1.0))
    bv = mult * ig * xc
    rows = lax.broadcasted_iota(I32, (tb, gw), 0)
    av = a
    d = stride
    while d < tb:
        a_sh = jnp.where(rows >= d, pltpu.roll(av, d, 0), 1.0)
        b_sh = jnp.where(rows >= d, pltpu.roll(bv, d, 0), 0.0)
        bv = av * b_sh + bv
        av = av * a_sh
        d *= 2
    hprev = hcar[...]
    if stride == 1:
        hfull = av * hprev + bv
    else:
        hfull = av * jnp.concatenate([hprev] * (tb // stride), axis=0) + bv
    hl = hfull[tb - stride:tb, :]
    hcar[...] = hl
    hlast_ref[0] = hl
    yabc_ref[0, :, 0:gw] = (hfull * _gelu(ga)).astype(BF16)

    u = _gelu(za[:, 2 * gw:3 * gw])
    vn = _layer_norm(_gelu(za[:, 3 * gw:4 * gw]), lng_ref[...], lnb_ref[...])
    if not prompt:
        vn_ref[0] = vn
    ghead = _lane_head((CHUNK, gw), gw // G_HEADS)
    for c in range(tb // CHUNK):
        crow = slice(c * CHUNK, (c + 1) * CHUNK)
        vnc = vn[crow].astype(BF16)
        mixed = bs_ref[...]
        for h in range(G_HEADS):
            mixed = mixed + jnp.where(ghead == h, _dot(ws_ref[h], vnc), 0.0)
        yabc_ref[0, crow, gw:2 * gw] = (u[crow] * mixed).astype(BF16)

    xp = za[:, 4 * gw:5 * gw]
    n0 = hp + tb
    pext[hp:n0, :] = xp
    s1, s2, s3, s4 = starts
    zero8 = jnp.zeros((max(s3, SUBLANES), gw), F32)
    l1[0:s1, :] = zero8[0:s1]
    l2[0:s2, :] = zero8[0:s2]
    l3[0:s3, :] = zero8[0:s3]
    l1[s1:n0, :] = pext[s1:n0, :] + pext[s1 - stride:n0 - stride, :]
    l2[s2:n0, :] = l1[s2:n0, :] + l1[s2 - 2 * stride:n0 - 2 * stride, :]
    l3[s3:n0, :] = l2[s3:n0, :] + l2[s3 - 4 * stride:n0 - 4 * stride, :]
    w16 = l3[hp:n0, :] + l3[hp - 8 * stride:n0 - 8 * stride, :]
    pgrp = _lane_head((tb, gw), gw // len(POOL_WINDOWS))
    sums = jnp.where(pgrp == 0, l1[hp:n0, :], jnp.where(pgrp == 1, l2[hp:n0, :],
                     jnp.where(pgrp == 2, l3[hp:n0, :], w16)))
    win = jnp.where(pgrp == 0, POOL_WINDOWS[0], jnp.where(pgrp == 1, POOL_WINDOWS[1],
                    jnp.where(pgrp == 2, POOL_WINDOWS[2], POOL_WINDOWS[3])))
    pos = start_pos + _div_pow2(tstep * tb + rows, stride)
    cnt = jnp.minimum(pos + 1, win).astype(F32)
    pooled = sums / cnt - xp
    yc = _dot(pooled.astype(BF16), poolw_ref[...]) * pscale_ref[...]
    yabc_ref[0, :, 2 * gw:3 * gw] = yc.astype(BF16)
    ptail = pext[n0 - hp:n0, :]
    ptail_ref[0] = ptail
    if tb >= hp:
        pext[0:hp, :] = ptail


def _block_diag(w):
    h, a, b = w.shape
    eye = jnp.eye(h, dtype=w.dtype)
    return (eye[:, None, :, None] * w[:, :, None, :]).reshape(h * a, h * b)


def _mixers(x, lw, conv0, pool0, h0, *, stride, tb, start_pos, prompt):
    nseq, rows, dm = x.shape
    gw = GROUP_W
    hc, hp, starts = _mixer_dims(stride)
    assert rows % tb == 0 and tb % CHUNK == 0 and tb % stride == 0
    nt = rows // tb
    n0 = hp + tb
    kern = functools.partial(_mixers_kernel, stride=stride, tb=tb, start_pos=start_pos, prompt=prompt)

    def const(shape):
        return pl.BlockSpec(shape, lambda b, t: (0,) * len(shape))

    def per_seq(shape):
        return pl.BlockSpec((1,) + shape, lambda b, t: (b,) + (0,) * len(shape))

    def per_blk(width):
        return pl.BlockSpec((1, tb, width), lambda b, t: (b, t, 0))

    in_specs = [per_blk(dm), const(lw["w_abc"].shape), const(lw["w_qkv"].shape), const(lw["w_idx"].shape),
                const(lw["w_kt"].shape), const((SUBLANES, gw)), const((1, gw)), const((gw, 2 * gw)),
                const((1, 2 * gw)), const((1, gw)), const((1, gw)), const((1, gw)),
                const((G_HEADS, CHUNK, CHUNK)), const((CHUNK, gw)), const((gw, gw)), const((1, gw)),
                per_seq((hc, gw)), per_seq((hp, gw)), per_seq((stride, gw))]
    out_shape = [jax.ShapeDtypeStruct((nseq, rows, 3 * gw), BF16),
                 jax.ShapeDtypeStruct((nseq, rows, gw), BF16),
                 jax.ShapeDtypeStruct((nseq, rows, gw), F32),
                 jax.ShapeDtypeStruct((nseq, rows, gw), F32),
                 jax.ShapeDtypeStruct((nseq, rows, IDX_HEADS * IDX_DIM), BF16),
                 jax.ShapeDtypeStruct((nseq, rows, LANES), F32),
                 jax.ShapeDtypeStruct((nseq, hc, gw), F32),
                 jax.ShapeDtypeStruct((nseq, hp, gw), F32),
                 jax.ShapeDtypeStruct((nseq, stride, gw), F32)]
    out_specs = [per_blk(3 * gw), per_blk(gw), per_blk(gw), per_blk(gw), per_blk(IDX_HEADS * IDX_DIM),
                 per_blk(LANES), per_seq((hc, gw)), per_seq((hp, gw)), per_seq((stride, gw))]
    if prompt:
        nck = rows // KEY_CHUNK
        cpb = tb // KEY_CHUNK
        out_shape += [jax.ShapeDtypeStruct((nseq, nck, gw, KEY_CHUNK), BF16),
                      jax.ShapeDtypeStruct((nseq, nck, IDX_DIM, KEY_CHUNK), BF16),
                      jax.ShapeDtypeStruct((nseq, nck, ATT_HEADS * KEY_CHUNK, gw), BF16)]
        out_specs += [pl.BlockSpec((1, cpb, gw, KEY_CHUNK), lambda b, t: (b, t, 0, 0)),
                      pl.BlockSpec((1, cpb, IDX_DIM, KEY_CHUNK), lambda b, t: (b, t, 0, 0)),
                      pl.BlockSpec((1, cpb, ATT_HEADS * KEY_CHUNK, gw), lambda b, t: (b, t, 0, 0))]
    else:
        out_shape += [jax.ShapeDtypeStruct((nseq, rows, gw), F32)]
        out_specs += [per_blk(gw)]
    scratch = [pltpu.VMEM((hc + tb, gw), F32), pltpu.VMEM((n0, gw), F32), pltpu.VMEM((n0, gw), F32),
               pltpu.VMEM((n0, gw), F32), pltpu.VMEM((n0, gw), F32), pltpu.VMEM((stride, gw), F32)]
    return pl.pallas_call(
        kern, out_shape=out_shape, grid=(nseq, nt), in_specs=in_specs, out_specs=out_specs,
        scratch_shapes=scratch, name="mixers_prompt" if prompt else "mixers_sample",
        compiler_params=pltpu.CompilerParams(dimension_semantics=("arbitrary", "arbitrary"),
                                             vmem_limit_bytes=VMEM_LIMIT),
    )(x, lw["w_abc"], lw["w_qkv"], lw["w_idx"], lw["w_kt"], lw["conv_w"], lw["conv_b"], lw["w_gate"],
      lw["b_gate"], lw["lam"], lw["ln_g"], lw["ln_b"], lw["ws_p" if prompt else "ws_s"],
      lw["bs_p" if prompt else "bs_s"], lw["pool_w"], lw["pool_scale"], conv0, pool0, h0)


def _count_rows(acc):
    return jnp.sum(acc, axis=1, keepdims=True)


def _kth_largest_key(count_ge, shape, topk):
    def step(it, base):
        bit = lax.shift_left(jnp.int32(1), 31 - it)
        cand = jnp.where(it == 0, jnp.int32(0), base | bit)
        return jnp.where(count_ge(cand) >= topk, cand, base)
    return lax.fori_loop(0, 32, step, jnp.full(shape, INT_MIN, I32))


def _upper_tri_ones():
    r = lax.broadcasted_iota(I32, (KEY_CHUNK, 2 * KEY_CHUNK), 0)
    c = lax.broadcasted_iota(I32, (KEY_CHUNK, 2 * KEY_CHUNK), 1)
    return jnp.where(r <= c, 1.0, 0.0).astype(BF16)


def _select_chunks(keys, thr, need, carry, tri):
    eqs = [(k == thr) & (k != INT_MIN) for k in keys]
    counts = [_dot(jnp.where(e, 1.0, 0.0).astype(BF16), tri) for e in eqs]
    sels = []
    for k, e, cnt in zip(keys, eqs, counts):
        sels.append((k > thr) | (e & (cnt[:, :KEY_CHUNK] + carry <= need)))
        carry = carry + cnt[:, KEY_CHUNK:]
    return sels, carry


def _dsa_prompt_kernel(b31_ref, qi_ref, kiwi_ref, q_ref, kit_ref, kt_ref, vbd_ref, btab_ref, out_ref,
                       qh_scr, wb_scr, qs_scr, s_scr, m_scr, l_scr, acc_scr, *, topk, qb):
    i = pl.program_id(1)
    gw = GROUP_W

    qi = qi_ref[0]
    kiwi = kiwi_ref[0]
    for h in range(IDX_HEADS):
        qh_scr[h * qb:(h + 1) * qb, :] = qi[:, h * IDX_DIM:(h + 1) * IDX_DIM]
        wcol = kiwi[:, IDX_DIM + h:IDX_DIM + h + 1] * IDX_SCALE
        wb_scr[h] = jnp.broadcast_to(wcol, (qb, LANES))
    qq = q_ref[0]
    for h in range(ATT_HEADS):
        qs_scr[h] = qq[:, h * HEAD_DIM:(h + 1) * HEAD_DIM]

    row = lax.broadcasted_iota(I32, (qb, KEY_CHUNK), 0)
    col = lax.broadcasted_iota(I32, (qb, KEY_CHUNK), 1)

    g_diag = i // CHUNK_GROUP

    def score_group(g, last):
        scores = []
        for cp in range(CHUNK_GROUP // 2):
            kit = jnp.concatenate([kit_ref[0, g * CHUNK_GROUP + 2 * cp + c] for c in range(2)], axis=1)
            sp = None
            for h in range(IDX_HEADS):
                rr = _dot(qh_scr[h * qb:(h + 1) * qb, :], kit)
                term = jnp.concatenate([wb_scr[h], wb_scr[h]], axis=1) * jnp.maximum(rr, 0.0)
                sp = term if sp is None else sp + term
            scores += [sp[:, :KEY_CHUNK], sp[:, KEY_CHUNK:]]
        for c in range(CHUNK_GROUP):
            j = g * CHUNK_GROUP + c
            key = _score_key(scores[c])
            if last:
                diag_key = jnp.where(col <= row, key, INT_MIN)
                key = jnp.where(j < i, key, jnp.where(j == i, diag_key, INT_MIN))
            s_scr[j] = key

    def score_body(g, c):
        score_group(g, False)
        return c
    lax.fori_loop(0, g_diag, score_body, 0)
    score_group(g_diag, True)

    def count(pred):
        def body(g, acc):
            for c in range(CHUNK_GROUP):
                acc = acc + jnp.where(pred(s_scr[g * CHUNK_GROUP + c]), 1.0, 0.0)
            return acc
        return _count_rows(lax.fori_loop(0, g_diag + 1, body, jnp.zeros((qb, KEY_CHUNK), F32)))

    thr = _kth_largest_key(lambda cand: count(lambda k: k >= cand), (qb, KEY_CHUNK), topk)
    need = jnp.broadcast_to(topk - count(lambda k: k > thr), (qb, KEY_CHUNK))

    m_scr[...] = jnp.full(m_scr.shape, NEG_BIG, F32)
    l_scr[...] = jnp.zeros(l_scr.shape, F32)
    acc_scr[...] = jnp.zeros(acc_scr.shape, F32)
    tri = _upper_tri_ones()
    g_far = jnp.maximum(i - 1, 0) // CHUNK_GROUP

    def group_logits(g, h):
        kt = jnp.concatenate([kt_ref[0, g * CHUNK_GROUP + c, pl.ds(h * HEAD_DIM, HEAD_DIM), :]
                              for c in range(CHUNK_GROUP)], axis=1)
        return _dot(qs_scr[h], kt)

    def near_bias(j, h):
        tab = btab_ref[h, jnp.clip(j - (i - 1), 0, 1)]
        return jnp.where(j >= i - 1, tab - b31_ref[h], 0.0)

    def max_group(g, carry, near):
        js = [g * CHUNK_GROUP + c for c in range(CHUNK_GROUP)]
        sels, carry = _select_chunks([s_scr[j] for j in js], thr, need, carry, tri)
        vmax = [None] * ATT_HEADS
        lg = [group_logits(g, h) for h in range(ATT_HEADS)]
        for c, j in enumerate(js):
            madd = jnp.where(sels[c], 0.0, NEG_BIG)
            s_scr[j] = lax.bitcast_convert_type(madd, I32)
            for h in range(ATT_HEADS):
                v = lg[h][:, c * KEY_CHUNK:(c + 1) * KEY_CHUNK] + madd
                if near:
                    v = v + near_bias(j, h)
                vmax[h] = v if vmax[h] is None else jnp.maximum(vmax[h], v)
        for h in range(ATT_HEADS):
            m_scr[h] = jnp.maximum(m_scr[h], vmax[h])
        return carry

    carry = lax.fori_loop(0, g_far, lambda g, c: max_group(g, c, False), jnp.zeros((qb, KEY_CHUNK), F32))
    lax.fori_loop(g_far, g_diag + 1, lambda g, c: max_group(g, c, True), carry)
    for h in range(ATT_HEADS):
        m_row = jnp.max(m_scr[h], axis=1, keepdims=True)
        m_scr[h] = jnp.broadcast_to(-m_row, (qb, LANES))

    def exp_group(g, near):
        lsum = [None] * ATT_HEADS
        lg = [group_logits(g, h) for h in range(ATT_HEADS)]
        ps = []
        for c in range(CHUNK_GROUP):
            j = g * CHUNK_GROUP + c
            madd = lax.bitcast_convert_type(s_scr[j], F32)
            for h in range(ATT_HEADS):
                shift = m_scr[h] + near_bias(j, h) if near else m_scr[h]
                p = jnp.exp(lg[h][:, c * KEY_CHUNK:(c + 1) * KEY_CHUNK] + (madd + shift))
                lsum[h] = p if lsum[h] is None else lsum[h] + p
                ps.append(p.astype(BF16))
        for h in range(ATT_HEADS):
            l_scr[h] = l_scr[h] + lsum[h]
        vg = vbd_ref[0, pl.ds(g * CHUNK_GROUP, CHUNK_GROUP)]
        acc_scr[...] = acc_scr[...] + _dot(jnp.concatenate(ps, axis=1),
                                           vg.reshape(CHUNK_GROUP * ATT_HEADS * KEY_CHUNK, gw))

    def far_body(g, c):
        exp_group(g, False)
        return c

    def near_body(g, c):
        exp_group(g, True)
        return c
    lax.fori_loop(0, g_far, far_body, 0)
    lax.fori_loop(g_far, g_diag + 1, near_body, 0)
    lane64 = lax.broadcasted_iota(I32, (qb, LANES), 1) < HEAD_DIM
    ls = [jnp.broadcast_to(jnp.sum(l_scr[h], axis=1, keepdims=True), (qb, LANES)) for h in range(ATT_HEADS)]
    l_cat = jnp.concatenate([jnp.where(lane64, ls[0], ls[1]), jnp.where(lane64, ls[2], ls[3])], axis=1)
    out_ref[0] = (acc_scr[...] / l_cat).astype(BF16)


def _dsa_prompt(qi, kiwi, q, kit, kt, vbd, btab, b31, *, topk):
    nb, t, _ = q.shape
    qb = KEY_CHUNK
    assert t % (qb * CHUNK_GROUP) == 0, "prompt length must be a multiple of a chunk group"
    nq = t // qb
    gw = GROUP_W
    kern = functools.partial(_dsa_prompt_kernel, topk=topk, qb=qb)

    def per_q(width):
        return pl.BlockSpec((1, qb, width), lambda b, i: (b, i, 0))

    def per_batch(shape):
        return pl.BlockSpec((1,) + shape, lambda b, i: (b,) + (0,) * len(shape))

    in_specs = [pl.BlockSpec(memory_space=pltpu.SMEM),
                per_q(IDX_HEADS * IDX_DIM), per_q(LANES), per_q(gw),
                per_batch((nq, IDX_DIM, KEY_CHUNK)), per_batch((nq, gw, KEY_CHUNK)),
                pl.BlockSpec((1, nq, ATT_HEADS * KEY_CHUNK, gw), lambda b, i: (b, 0, 0, 0),
                             pipeline_mode=pl.Buffered(1)),
                pl.BlockSpec((ATT_HEADS, 2, KEY_CHUNK, KEY_CHUNK), lambda b, i: (0, 0, 0, 0))]
    scratch = [pltpu.VMEM((IDX_HEADS * qb, IDX_DIM), BF16), pltpu.VMEM((IDX_HEADS, qb, LANES), F32),
               pltpu.VMEM((ATT_HEADS, qb, HEAD_DIM), BF16), pltpu.VMEM((nq, qb, KEY_CHUNK), I32),
               pltpu.VMEM((ATT_HEADS, qb, LANES), F32), pltpu.VMEM((ATT_HEADS, qb, LANES), F32),
               pltpu.VMEM((qb, gw), F32)]
    return pl.pallas_call(
        kern, out_shape=jax.ShapeDtypeStruct((nb, t, gw), BF16), grid=(nb, nq),
        in_specs=in_specs, out_specs=per_q(gw), scratch_shapes=scratch, name="dsa_prompt",
        compiler_params=pltpu.CompilerParams(dimension_semantics=("arbitrary", "arbitrary"),
                                             vmem_limit_bytes=VMEM_LIMIT),
    )(b31, qi, kiwi, q, kit, kt, vbd, btab)


PAGES_PER_STEP = 16
SAMPLE_COUNT_UNROLL = 8


def _sample_scores(qi, wpad, keys_t):
    rr = _dot(qi, keys_t)
    s = jnp.zeros((SUBLANES, rr.shape[1]), F32)
    for h in range(IDX_HEADS):
        rows = slice(h * SUBLANES, (h + 1) * SUBLANES)
        s = s + wpad[rows, 0:1] * jnp.maximum(rr[rows], 0.0)
    return _score_key(s)


def _dsa_sample_index_kernel(pt_ref, qi_ref, wpad_ref, *refs):
    g = PAGES_PER_STEP
    page_refs = refs[:g]
    kinew_ref, spast_ref, snew_ref = refs[g:]
    s = pl.program_id(1)
    qi = qi_ref[0]
    wpad = wpad_ref[0]
    pages = jnp.concatenate([page_refs[p][0, 0].astype(BF16) for p in range(g)], axis=1)
    keys = _sample_scores(qi, wpad, pages)
    for p in range(g):
        spast_ref[0, s * g + p] = keys[:, p * KEY_CHUNK:(p + 1) * KEY_CHUNK]

    @pl.when(s == pl.num_programs(1) - 1)
    def _():
        row = lax.broadcasted_iota(I32, (SUBLANES, KEY_CHUNK), 0)
        col = lax.broadcasted_iota(I32, (SUBLANES, KEY_CHUNK), 1)
        snew_ref[0] = jnp.where(col <= row, _sample_scores(qi, wpad, kinew_ref[0]), INT_MIN)


def _sample_threshold_kernel(keys_ref, thr_ref, need_ref, *, npages, topk):
    shape = thr_ref.shape
    knew = keys_ref[npages]

    def count(pred):
        def body(jj, acc):
            for c in range(SAMPLE_COUNT_UNROLL):
                acc = acc + jnp.where(pred(keys_ref[jj * SAMPLE_COUNT_UNROLL + c]), 1.0, 0.0)
            return acc
        acc = lax.fori_loop(0, npages // SAMPLE_COUNT_UNROLL, body, jnp.where(pred(knew), 1.0, 0.0))
        return _count_rows(acc)

    thr = _kth_largest_key(lambda cand: count(lambda k: k >= cand), shape, topk)
    thr_ref[...] = thr
    need_ref[...] = jnp.broadcast_to(topk - count(lambda k: k > thr), shape)


def _dsa_sample_attend_kernel(pt_ref, qbd_ref, spast_ref, snew_ref, thr_ref, need_ref, *refs, npages):
    g = PAGES_PER_STEP
    k_refs = refs[:g]
    v_refs = refs[g:2 * g]
    knew_ref, vnew_ref, tabs_ref, out_ref, m_scr, l_scr, acc_scr, carry_scr = refs[2 * g:]
    s = pl.program_id(1)
    nh = ATT_HEADS * SUBLANES

    @pl.when(s == 0)
    def _():
        m_scr[...] = jnp.full(m_scr.shape, NEG_BIG, F32)
        l_scr[...] = jnp.zeros(l_scr.shape, F32)
        acc_scr[...] = jnp.zeros(acc_scr.shape, F32)
        carry_scr[...] = jnp.zeros(carry_scr.shape, F32)

    qbd = qbd_ref[0]
    thr = thr_ref[0]
    need = need_ref[0]
    tri = _upper_tri_ones()

    def select(keys):
        sels, carry = _select_chunks(keys, thr, need, carry_scr[...], tri)
        carry_scr[...] = carry
        return [jnp.concatenate([jnp.where(sel8, 1, 0)] * ATT_HEADS, axis=0) for sel8 in sels]

    def attend(sel, kt, vt, bias):
        lg = jnp.where(sel > 0, _dot(qbd, kt) + bias, NEG_BIG)
        m_old = m_scr[...]
        m_new = jnp.maximum(m_old, jnp.max(lg, axis=1, keepdims=True))
        alpha = jnp.exp(m_old - m_new)
        p = jnp.where(sel > 0, jnp.exp(lg - m_new[:, 0:1]), 0.0)
        l_scr[...] = alpha * l_scr[...] + jnp.sum(p, axis=1, keepdims=True)
        m_scr[...] = m_new
        alpha2 = jnp.concatenate([alpha, alpha], axis=1)
        acc_scr[...] = acc_scr[...] * alpha2 + _dot_nt(p.astype(BF16), vt)

    biases = [jnp.where(s * g + p == npages - 1, tabs_ref[0], tabs_ref[2]) for p in range(g)]
    attend(jnp.concatenate(select([spast_ref[0, p] for p in range(g)]), axis=1),
           jnp.concatenate([k_refs[p][0, 0].astype(BF16) for p in range(g)], axis=1),
           jnp.concatenate([v_refs[p][0, 0].astype(BF16) for p in range(g)], axis=1),
           jnp.concatenate(biases, axis=1))

    @pl.when(s == pl.num_programs(1) - 1)
    def _():
        attend(select([snew_ref[0]])[0], knew_ref[0], vnew_ref[0], tabs_ref[1])
        l2 = jnp.concatenate([l_scr[...], l_scr[...]], axis=1)
        o = acc_scr[...] / l2
        head = _lane_head((SUBLANES, GROUP_W), HEAD_DIM)
        out = jnp.zeros((SUBLANES, GROUP_W), F32)
        for h in range(ATT_HEADS):
            out = out + jnp.where(head == h, o[h * SUBLANES:(h + 1) * SUBLANES], 0.0)
        out_ref[0] = out


def _dsa_sample(layer, page_table, qi_pad, wpad, kinew, qbd, knew, vnew, cache_ik, cache_k, cache_v, tabs, *,
                topk, ts):
    db, npages = page_table.shape
    g = PAGES_PER_STEP
    assert npages % g == 0 and npages % SAMPLE_COUNT_UNROLL == 0 and (db * ts) % SUBLANES == 0
    ns = npages // g
    gw = GROUP_W
    pt = page_table.reshape(-1)

    def page_spec(width, p):
        return pl.BlockSpec((1, 1, width, PAGE_SIZE),
                            lambda b, s, pt_ref: (layer, pt_ref[b * npages + s * g + p], 0, 0))

    def per_b(shape):
        return pl.BlockSpec((1,) + shape, lambda b, s, pt_ref: (b,) + (0,) * len(shape))

    spast, snew = pl.pallas_call(
        _dsa_sample_index_kernel,
        out_shape=(jax.ShapeDtypeStruct((db, npages, SUBLANES, KEY_CHUNK), I32),
                   jax.ShapeDtypeStruct((db, SUBLANES, KEY_CHUNK), I32)),
        grid_spec=pltpu.PrefetchScalarGridSpec(
            num_scalar_prefetch=1, grid=(db, ns),
            in_specs=[per_b((IDX_HEADS * SUBLANES, IDX_DIM)), per_b((IDX_HEADS * SUBLANES, LANES))]
                     + [page_spec(IDX_DIM, p) for p in range(g)] + [per_b((IDX_DIM, KEY_CHUNK))],
            out_specs=[per_b((npages, SUBLANES, KEY_CHUNK)), per_b((SUBLANES, KEY_CHUNK))]),
        name="dsa_sample_index",
        compiler_params=pltpu.CompilerParams(dimension_semantics=("arbitrary", "arbitrary"),
                                             vmem_limit_bytes=VMEM_LIMIT),
    )(pt, qi_pad, wpad, *([cache_ik] * g), kinew)

    rows = db * ts
    keys_all = jnp.concatenate([jnp.swapaxes(spast[:, :, :ts], 0, 1).reshape(npages, rows, KEY_CHUNK),
                                snew[:, :ts].reshape(1, rows, KEY_CHUNK)], axis=0)
    thr_rows, need_rows = pl.pallas_call(
        functools.partial(_sample_threshold_kernel, npages=npages, topk=topk),
        out_shape=(jax.ShapeDtypeStruct((rows, KEY_CHUNK), I32), jax.ShapeDtypeStruct((rows, KEY_CHUNK), F32)),
        name="dsa_sample_threshold",
        compiler_params=pltpu.CompilerParams(vmem_limit_bytes=VMEM_LIMIT),
    )(keys_all)
    pad_rows = lambda a, fill: jnp.pad(a.reshape(db, ts, KEY_CHUNK), ((0, 0), (0, SUBLANES - ts), (0, 0)),
                                       constant_values=fill)
    thr, need = pad_rows(thr_rows, 0), pad_rows(need_rows, float(topk))

    attend_kern = functools.partial(_dsa_sample_attend_kernel, npages=npages)
    nh = ATT_HEADS * SUBLANES
    return pl.pallas_call(
        attend_kern,
        out_shape=jax.ShapeDtypeStruct((db, SUBLANES, gw), F32),
        grid_spec=pltpu.PrefetchScalarGridSpec(
            num_scalar_prefetch=1, grid=(db, ns),
            in_specs=[per_b((nh, gw)),
                      pl.BlockSpec((1, g, SUBLANES, KEY_CHUNK), lambda b, s, pt_ref: (b, s, 0, 0)),
                      per_b((SUBLANES, KEY_CHUNK)), per_b((SUBLANES, KEY_CHUNK)), per_b((SUBLANES, KEY_CHUNK))]
                     + [page_spec(gw, p) for p in range(g)] + [page_spec(gw, p) for p in range(g)]
                     + [per_b((gw, KEY_CHUNK)), per_b((gw, KEY_CHUNK)),
                        pl.BlockSpec((3, nh, KEY_CHUNK), lambda b, s, pt_ref: (0, 0, 0))],
            out_specs=per_b((SUBLANES, gw)),
            scratch_shapes=[pltpu.VMEM((nh, LANES), F32), pltpu.VMEM((nh, LANES), F32),
                            pltpu.VMEM((nh, gw), F32), pltpu.VMEM((SUBLANES, KEY_CHUNK), F32)]),
        name="dsa_sample_attend",
        compiler_params=pltpu.CompilerParams(dimension_semantics=("arbitrary", "arbitrary"),
                                             vmem_limit_bytes=VMEM_LIMIT),
    )(pt, qbd, spast, snew, thr, need, *([cache_k] * g), *([cache_v] * g), knew, vnew, tabs)


FF_CHUNK = 1024


def _ffn_kernel(x_ref, yabc_ref, yd_ref, woa_ref, wod_ref, g1_ref, b1_ref, w1_ref, w2_ref, g2_ref, b2_ref,
                out_ref, *, alpha):
    x = x_ref[...]
    mix = _dot(yabc_ref[...], woa_ref[...]) + _dot(yd_ref[...], wod_ref[...])
    x1 = _layer_norm(alpha * x + mix, g1_ref[...], b1_ref[...])
    x1b = x1.astype(BF16)
    acc = jnp.zeros(x.shape, F32)
    for c in range(w1_ref.shape[1] // FF_CHUNK):
        cols = slice(c * FF_CHUNK, (c + 1) * FF_CHUNK)
        hdn = jnp.square(jnp.maximum(_dot(x1b, w1_ref[:, cols]), 0.0))
        acc = acc + _dot(hdn.astype(BF16), w2_ref[cols, :])
    out_ref[...] = _layer_norm(alpha * x1 + acc, g2_ref[...], b2_ref[...])


def _ffn(x, yabc, yd, lw, *, tm, alpha):
    rows, dm = x.shape
    dff = lw["w1"].shape[1]
    assert rows % tm == 0 and dff % FF_CHUNK == 0

    def blk(width):
        return pl.BlockSpec((tm, width), lambda r: (r, 0))

    def const(shape):
        return pl.BlockSpec(shape, lambda r: (0,) * len(shape), pipeline_mode=pl.Buffered(1))

    return pl.pallas_call(
        functools.partial(_ffn_kernel, alpha=alpha),
        out_shape=jax.ShapeDtypeStruct((rows, dm), F32), grid=(rows // tm,),
        in_specs=[blk(dm), blk(3 * GROUP_W), blk(GROUP_W), const((3 * GROUP_W, dm)), const((GROUP_W, dm)),
                  const((1, dm)), const((1, dm)), const((dm, dff)), const((dff, dm)), const((1, dm)),
                  const((1, dm))],
        out_specs=blk(dm), name="ffn",
        compiler_params=pltpu.CompilerParams(dimension_semantics=("arbitrary",),
                                             vmem_limit_bytes=VMEM_LIMIT),
    )(x, yabc, yd, lw["w_out_abc"], lw["w_out_d"], lw["ln1_g"], lw["ln1_b"], lw["w1"], lw["w2"],
      lw["ln2_g"], lw["ln2_b"])


def _layer_weights(l, dec_batch, dec_seq, w_in, conv_w, conv_b, lru_wa, lru_ba, lru_wx, lru_bx, lru_lambda,
                   gmlp_ln_g, gmlp_ln_b, gmlp_ws, gmlp_bs, pool_w, pool_scale, w_out, ln1_g, ln1_b, ln2_g,
                   ln2_b, ffn_w1, ffn_w2):
    gw = GROUP_W
    dm = w_in.shape[1]
    wi = w_in[l]
    o_q, o_k, o_v = 5 * gw, 6 * gw, 7 * gw
    o_qi = 8 * gw
    o_ki = o_qi + IDX_HEADS * IDX_DIM
    o_wi = o_ki + IDX_DIM
    pad = jnp.zeros((dm, LANES - IDX_DIM - IDX_HEADS), wi.dtype)
    row = lambda a: a[l].reshape(1, -1)
    tril = jnp.tril(jnp.ones((CHUNK, CHUNK), F32))
    ws_tril = gmlp_ws[l] * tril
    eye_b = jnp.eye(dec_batch, dtype=F32)
    ws_small = ws_tril[:, :dec_seq, :dec_seq]
    ws_s = (ws_small[:, :, None, :, None] * eye_b[None, None, :, None, :]).reshape(
        G_HEADS, dec_seq * dec_batch, dec_seq * dec_batch)
    bs_cols = jnp.repeat(gmlp_bs[l].T, gw // G_HEADS, axis=1)
    return {
        "w_abc": wi[:, :o_q].astype(BF16),
        "w_qkv": wi[:, o_q:o_qi].astype(BF16),
        "w_idx": jnp.concatenate([wi[:, o_qi:], pad], axis=1).astype(BF16),
        "w_kt": jnp.concatenate([wi[:, o_k:o_v], wi[:, o_ki:o_wi]], axis=1).T.astype(BF16),
        "conv_w": jnp.concatenate([conv_w[l], jnp.zeros((SUBLANES - CONV_W, gw), F32)], axis=0),
        "conv_b": row(conv_b),
        "w_gate": jnp.concatenate([_block_diag(lru_wa[l]), _block_diag(lru_wx[l])], axis=1).astype(BF16),
        "b_gate": jnp.concatenate([lru_ba[l], lru_bx[l]]).reshape(1, -1),
        "lam": row(lru_lambda), "ln_g": row(gmlp_ln_g), "ln_b": row(gmlp_ln_b),
        "ws_p": ws_tril.astype(BF16), "ws_s": ws_s.astype(BF16),
        "bs_p": bs_cols, "bs_s": jnp.repeat(bs_cols[:dec_seq], dec_batch, axis=0),
        "pool_w": _block_diag(pool_w[l]).astype(BF16), "pool_scale": row(pool_scale),
        "w_out_abc": w_out[l, :3 * gw].astype(BF16), "w_out_d": w_out[l, 3 * gw:].astype(BF16),
        "ln1_g": row(ln1_g), "ln1_b": row(ln1_b), "ln2_g": row(ln2_g), "ln2_b": row(ln2_b),
        "w1": ffn_w1[l].astype(BF16), "w2": ffn_w2[l].astype(BF16),
    }


PROMPT_TB = 256
FFN_TM = 512


def kernel(x_prompt, x_sample, cache_k, cache_v, cache_idx_k, state_lru_h, state_conv, state_pool, page_table,
           w_in, conv_w, conv_b, lru_wa, lru_ba, lru_wx, lru_bx, lru_lambda, gmlp_ln_g, gmlp_ln_b, gmlp_ws,
           gmlp_bs, pool_w, pool_scale, rel_bias, w_out, ln1_g, ln1_b, ln2_g, ln2_b, ffn_w1, ffn_w2):
    nb, t, dm = x_prompt.shape
    db, ts, _ = x_sample.shape
    depth = w_in.shape[0]
    gw = GROUP_W
    npages = page_table.shape[1]
    past_len = npages * PAGE_SIZE
    topk_p = min(TOPK_MAX, t // 4)
    topk_s = min(TOPK_MAX, (past_len + ts) // 4)
    alpha = (2 * depth) ** 0.25
    n_pool = cache_k.shape[1]
    cache_kt = jnp.transpose(cache_k, (0, 1, 3, 4, 2)).reshape(depth, n_pool, gw, PAGE_SIZE)
    cache_vt = jnp.transpose(cache_v, (0, 1, 3, 4, 2)).reshape(depth, n_pool, gw, PAGE_SIZE)
    cache_ikt = jnp.swapaxes(cache_idx_k, 2, 3)

    btab, tabs = _bias_tables(rel_bias)
    b31 = rel_bias[NUM_BUCKETS - 1]
    hc_p, hp_p, _ = _mixer_dims(1)
    hc_s, hp_s, _ = _mixer_dims(db)
    srows = ts * db

    def to_rows(a):
        return jnp.swapaxes(a, 0, 1).reshape((srows,) + a.shape[2:])

    def from_rows(a):
        return jnp.swapaxes(a.reshape((ts, db) + a.shape[1:]), 0, 1)

    xp = x_prompt
    xs = to_rows(x_sample)[None]
    st_p, st_s = [], []
    for l in range(depth):
        lw = _layer_weights(l, db, ts, w_in, conv_w, conv_b, lru_wa, lru_ba, lru_wx, lru_bx, lru_lambda,
                            gmlp_ln_g, gmlp_ln_b, gmlp_ws, gmlp_bs, pool_w, pool_scale, w_out, ln1_g, ln1_b,
                            ln2_g, ln2_b, ffn_w1, ffn_w2)
        (yabc, q, k, v, qi, kiwi, ctail, ptail, hlast, kt, kit, vbd) = _mixers(
            xp, lw, jnp.zeros((nb, hc_p, gw), F32), jnp.zeros((nb, hp_p, gw), F32), jnp.zeros((nb, 1, gw), F32),
            stride=1, tb=PROMPT_TB, start_pos=0, prompt=True)
        yd = _dsa_prompt(qi, kiwi, q, kit, kt, vbd, btab, b31, topk=topk_p)
        xp = _ffn(xp.reshape(nb * t, dm), yabc.reshape(nb * t, 3 * gw), yd.reshape(nb * t, gw), lw,
                  tm=FFN_TM, alpha=alpha).reshape(nb, t, dm)
        st_p.append((k.reshape(nb, t, ATT_HEADS, HEAD_DIM), v.reshape(nb, t, ATT_HEADS, HEAD_DIM),
                     kiwi[:, :, :IDX_DIM], hlast[:, 0], ctail[:, hc_p - (CONV_W - 1):],
                     ptail[:, hp_p - (POOL_MAX - 1):]))
        conv0 = jnp.swapaxes(state_conv[l], 0, 1).reshape(1, hc_s, gw)
        pool0 = jnp.swapaxes(state_pool[l], 0, 1).reshape(1, hp_s, gw)
        (yabc_s, q_s, k_s, v_s, qi_s, kiwi_s, ctail_s, ptail_s, hlast_s, vn_s) = _mixers(
            xs, lw, conv0, pool0, state_lru_h[l][None], stride=db, tb=srows, start_pos=past_len, prompt=False)
        k_new = from_rows(k_s[0])
        v_new = from_rows(v_s[0])
        ki_new = from_rows(kiwi_s[0, :, :IDX_DIM])
        w_new = from_rows(kiwi_s[0, :, IDX_DIM:IDX_DIM + IDX_HEADS])
        qi_new = from_rows(qi_s[0]).reshape(db, ts, IDX_HEADS, IDX_DIM)
        q_new = from_rows(q_s[0]).reshape(db, ts, ATT_HEADS, HEAD_DIM)
        tpad = SUBLANES - ts
        qi_pad = jnp.pad(jnp.swapaxes(qi_new, 1, 2), ((0, 0), (0, 0), (0, tpad), (0, 0))).reshape(
            db, IDX_HEADS * SUBLANES, IDX_DIM)
        wpad = jnp.pad(jnp.swapaxes(w_new, 1, 2) * IDX_SCALE, ((0, 0), (0, 0), (0, tpad))).reshape(
            db, IDX_HEADS * SUBLANES, 1)
        wpad = jnp.broadcast_to(wpad, (db, IDX_HEADS * SUBLANES, LANES))
        q_heads = jnp.pad(jnp.swapaxes(q_new, 1, 2), ((0, 0), (0, 0), (0, tpad), (0, 0)))
        eye_h = jnp.eye(ATT_HEADS, dtype=q_heads.dtype)
        qbd = (q_heads[:, :, :, None, :] * eye_h[None, :, None, :, None]).reshape(
            db, ATT_HEADS * SUBLANES, gw)
        new_t = lambda a: jnp.swapaxes(jnp.pad(a, ((0, 0), (0, KEY_CHUNK - ts), (0, 0))), 1, 2).astype(BF16)
        yd_s = _dsa_sample(l, page_table, qi_pad, wpad, new_t(ki_new), qbd, new_t(k_new), new_t(v_new),
                           cache_ikt, cache_kt, cache_vt, tabs, topk=topk_s, ts=ts)
        yd_rows = to_rows(yd_s[:, :ts]).astype(BF16)
        xs = _ffn(xs[0], yabc_s[0], yd_rows, lw, tm=srows, alpha=alpha)[None]
        st_s.append((k_new.reshape(db, ts, ATT_HEADS, HEAD_DIM), v_new.reshape(db, ts, ATT_HEADS, HEAD_DIM),
                     ki_new, hlast_s[0],
                     jnp.swapaxes(ctail_s[0].reshape(CONV_W - 1, db, gw), 0, 1),
                     jnp.swapaxes(ptail_s[0].reshape(POOL_MAX - 1, db, gw), 0, 1),
                     from_rows(vn_s[0])))

    stack = lambda sts, j: jnp.stack([s[j] for s in sts], axis=0)
    return (xp, from_rows(xs[0]),
            stack(st_p, 0), stack(st_p, 1), stack(st_p, 2), stack(st_p, 3), stack(st_p, 4), stack(st_p, 5),
            stack(st_s, 0), stack(st_s, 1), stack(st_s, 2), stack(st_s, 3), stack(st_s, 4), stack(st_s, 5),
            stack(st_s, 6))
```
